```python
import math
import jax
import jax.numpy as jnp
from jax import lax
import numpy as np

D_MODEL = 1024
BATCH = 8
SEQ = 4096
DEPTH = 4

GRID_W = 64
CTX_LEN = 256

POOL_WIDTH = 512
POOL_GROUPS = 4
POOL_GROUP_DIM = POOL_WIDTH // POOL_GROUPS
POOL_WINDOWS = (2, 4, 8, 16)

SSM_WIDTH = 512
SSM_GROUP_DIM = 16
SSM_GROUPS = SSM_WIDTH // SSM_GROUP_DIM
SSM_STATE = 64
N_DIRECTIONS = 2

N_BRANCHES = 2
IN_WIDTH = POOL_WIDTH + SSM_WIDTH + N_BRANCHES * D_MODEL
N_MOD = 6

N_EXPERTS = 16
N_EXPERT_GROUPS = 4
EXPERTS_PER_GROUP = N_EXPERTS // N_EXPERT_GROUPS
GROUP_SCORE_K = 2
TOP_K = 2
D_EXPERT = 512

RMS_EPS = 1e-6
DT_MIN = 1e-3
DT_MAX = 1e-1

kernel_name = "hybrid_pool_s5_moe_prefix_dit"


def _rms_norm(x, g):
    xf = x.astype(jnp.float32)
    y = xf * lax.rsqrt(jnp.mean(xf * xf, axis=-1, keepdims=True) + RMS_EPS)
    return (y * g.astype(jnp.float32)).astype(x.dtype)


def _modulate(h, shift, scale):
    return h * (1.0 + scale) + shift


def _adaln_params(s, w, b):
    m = s @ w + b
    return jnp.split(m[:, None, :], N_MOD, axis=-1)


def _split_in(z):
    u_pool = z[..., :POOL_WIDTH]
    u_ssm = z[..., POOL_WIDTH:POOL_WIDTH + SSM_WIDTH]
    g_pool = z[..., POOL_WIDTH + SSM_WIDTH:POOL_WIDTH + SSM_WIDTH + D_MODEL]
    g_ssm = z[..., POOL_WIDTH + SSM_WIDTH + D_MODEL:]
    return u_pool, u_ssm, g_pool, g_ssm


def _box_mean(v, w, axis):
    n = v.shape[axis]
    vf = v.astype(jnp.float32)
    cs = jnp.cumsum(vf, axis=axis)
    pad_shape = list(vf.shape)
    pad_shape[axis] = 1
    cs = jnp.concatenate([jnp.zeros(pad_shape, jnp.float32), cs], axis=axis)
    t = np.arange(n)
    lo = np.clip(t - w // 2, 0, n)
    hi = np.clip(t - w // 2 + w, 0, n)
    total = jnp.take(cs, hi, axis=axis) - jnp.take(cs, lo, axis=axis)
    cnt_shape = [1] * vf.ndim
    cnt_shape[axis] = n
    cnt = jnp.asarray((hi - lo).reshape(cnt_shape), jnp.float32)
    return total / cnt


def _pool_mixer(u, pool_w, pool_scale, grid_rows):
    b, n, _ = u.shape
    ug = u.reshape(b, n, POOL_GROUPS, POOL_GROUP_DIM)
    outs = []
    for gi, w in enumerate(POOL_WINDOWS):
        v = ug[:, :, gi]
        if grid_rows is not None:
            v2 = v.reshape(b, grid_rows, GRID_W, POOL_GROUP_DIM)
            m = _box_mean(_box_mean(v2, w, 1), w, 2).reshape(b, n, POOL_GROUP_DIM)
        else:
            m = _box_mean(v, w, 1)
        outs.append(m - v.astype(jnp.float32))
    p = jnp.stack(outs, axis=2).astype(u.dtype)
    y = jnp.einsum('blgc,gcd->blgd', p, pool_w).reshape(b, n, POOL_WIDTH)
    return y * pool_scale


def _zoh(a_re, a_im, log_dt, b_re, b_im):
    a_re = a_re.astype(jnp.float32)
    a_im = a_im.astype(jnp.float32)
    b_re = b_re.astype(jnp.float32)
    b_im = b_im.astype(jnp.float32)
    dt = jnp.exp(log_dt.astype(jnp.float32))[:, None]
    mag = jnp.exp(a_re * dt)
    abar_re = mag * jnp.cos(a_im * dt)
    abar_im = mag * jnp.sin(a_im * dt)
    nr = abar_re - 1.0
    ni = abar_im
    den = a_re * a_re + a_im * a_im
    f_re = (nr * a_re + ni * a_im) / den
    f_im = (ni * a_re - nr * a_im) / den
    bbar_re = f_re[..., None] * b_re - f_im[..., None] * b_im
    bbar_im = f_re[..., None] * b_im + f_im[..., None] * b_re
    return abar_re, abar_im, bbar_re, bbar_im


def _complex_affine_combine(e1, e2):
    a1r, a1i, b1r, b1i = e1
    a2r, a2i, b2r, b2i = e2
    ar = a1r * a2r - a1i * a2i
    ai = a1r * a2i + a1i * a2r
    br = a2r * b1r - a2i * b1i + b2r
    bi = a2r * b1i + a2i * b1r + b2i
    return ar, ai, br, bi


def _ssm_scan(u_g, abar_re, abar_im, bbar_re, bbar_im, h0_re, h0_im, reverse):
    bu_re = jnp.einsum('blgj,gpj->lbgp', u_g, bbar_re)
    bu_im = jnp.einsum('blgj,gpj->lbgp', u_g, bbar_im)
    if reverse:
        bu_re = bu_re[::-1]
        bu_im = bu_im[::-1]
    bu_re = bu_re.at[0].add(abar_re * h0_re - abar_im * h0_im)
    bu_im = bu_im.at[0].add(abar_re * h0_im + abar_im * h0_re)
    n = bu_re.shape[0]
    a_re_t = jnp.broadcast_to(abar_re, (n, 1) + abar_re.shape)
    a_im_t = jnp.broadcast_to(abar_im, (n, 1) + abar_im.shape)
    _, _, s_re, s_im = lax.associative_scan(
        _complex_affine_combine, (a_re_t, a_im_t, bu_re, bu_im), axis=0)
    fin_re = s_re[-1]
    fin_im = s_im[-1]
    if reverse:
        s_re = s_re[::-1]
        s_im = s_im[::-1]
    return s_re, s_im, fin_re, fin_im


def _readout(s_re, s_im, c_re, c_im):
    c_re = c_re.astype(jnp.float32)
    c_im = c_im.astype(jnp.float32)
    return (jnp.einsum('lbgp,gjp->blgj', s_re, c_re)
            - jnp.einsum('lbgp,gjp->blgj', s_im, c_im))


def _s5_glu(y, glu_w, glu_b):
    b, n = y.shape[0], y.shape[1]
    g = jax.nn.gelu(y.reshape(b, n, SSM_WIDTH))
    return g * jax.nn.sigmoid(g @ glu_w + glu_b)


def _s5_mixer(u_lat, u_ctx, need_ctx, a_re, a_im, log_dt, b_re, b_im, c_re, c_im,
              d_skip, glu_w, glu_b):
    bsz, n_lat, _ = u_lat.shape
    n_ctx = u_ctx.shape[1]
    ug_lat = u_lat.astype(jnp.float32).reshape(bsz, n_lat, SSM_GROUPS, SSM_GROUP_DIM)
    ug_ctx = u_ctx.astype(jnp.float32).reshape(bsz, n_ctx, SSM_GROUPS, SSM_GROUP_DIM)
    d_g = d_skip.astype(jnp.float32).reshape(SSM_GROUPS, SSM_GROUP_DIM)
    y_lat = ug_lat * d_g
    y_ctx = ug_ctx * d_g if need_ctx else None
    zero = jnp.zeros((bsz, SSM_GROUPS, SSM_STATE), jnp.float32)
    for direction in range(N_DIRECTIONS):
        rev = direction == 1
        abar_re, abar_im, bbar_re, bbar_im = _zoh(
            a_re[direction], a_im[direction], log_dt[direction], b_re[direction], b_im[direction])
        sc_re, sc_im, fin_re, fin_im = _ssm_scan(
            ug_ctx, abar_re, abar_im, bbar_re, bbar_im, zero, zero, rev)
        sl_re, sl_im, _, _ = _ssm_scan(
            ug_lat, abar_re, abar_im, bbar_re, bbar_im, fin_re, fin_im, rev)
        y_lat = y_lat + _readout(sl_re, sl_im, c_re[direction], c_im[direction])
        if need_ctx:
            y_ctx = y_ctx + _readout(sc_re, sc_im, c_re[direction], c_im[direction])
    out_lat = _s5_glu(y_lat, glu_w, glu_b).astype(u_lat.dtype)
    out_ctx = _s5_glu(y_ctx, glu_w, glu_b).astype(u_ctx.dtype) if need_ctx else None
    return out_lat, out_ctx


def _merge(p, s, g_pool, g_ssm, w_bpool, w_bssm, w_out):
    m = jax.nn.sigmoid(g_pool) * (p @ w_bpool) + jax.nn.sigmoid(g_ssm) * (s @ w_bssm)
    return m @ w_out


def _token_mixer(h_lat, h_ctx, need_ctx, w_in, pool_w, pool_scale, a_re, a_im, log_dt,
                 b_re, b_im, c_re, c_im, d_skip, glu_w, glu_b, w_bpool, w_bssm, w_out):
    rows = h_lat.shape[1] // GRID_W
    u_pool, u_ssm, g_pool, g_ssm = _split_in(h_lat @ w_in)
    if need_ctx:
        cu_pool, cu_ssm, cg_pool, cg_ssm = _split_in(h_ctx @ w_in)
    else:
        cu_ssm = h_ctx @ w_in[:, POOL_WIDTH:POOL_WIDTH + SSM_WIDTH]
    s_lat, s_ctx = _s5_mixer(u_ssm, cu_ssm, need_ctx, a_re, a_im, log_dt, b_re, b_im,
                             c_re, c_im, d_skip, glu_w, glu_b)
    p_lat = _pool_mixer(u_pool, pool_w, pool_scale, rows)
    out_lat = _merge(p_lat, s_lat, g_pool, g_ssm, w_bpool, w_bssm, w_out).astype(h_lat.dtype)
    out_ctx = None
    if need_ctx:
        p_ctx = _pool_mixer(cu_pool, pool_w, pool_scale, None)
        out_ctx = _merge(p_ctx, s_ctx, cg_pool, cg_ssm, w_bpool, w_bssm, w_out).astype(h_ctx.dtype)
    return out_lat, out_ctx


def _moe(h, router_w, router_b, w1, w3, w2):
    b, n, d = h.shape
    t = h.reshape(b * n, d)
    scores = jax.nn.sigmoid((t @ router_w).astype(jnp.float32))
    biased = scores + router_b.astype(jnp.float32)
    grouped = biased.reshape(-1, N_EXPERT_GROUPS, EXPERTS_PER_GROUP)
    group_score = jnp.sum(lax.top_k(grouped, GROUP_SCORE_K)[0], axis=-1)
    best_group = jnp.argmax(group_score, axis=-1)
    expert_group = jnp.arange(N_EXPERTS) // EXPERTS_PER_GROUP
    in_group = expert_group[None, :] == best_group[:, None]
    masked = jnp.where(in_group, biased, -jnp.inf)
    _, idx = lax.top_k(masked, TOP_K)
    w_sel = jnp.take_along_axis(scores, idx, axis=-1)
    w_sel = w_sel / jnp.sum(w_sel, axis=-1, keepdims=True)
    gates = jnp.sum(jax.nn.one_hot(idx, N_EXPERTS, dtype=jnp.float32) * w_sel[..., None], axis=1)
    gates = gates.astype(t.dtype)
    out = jnp.zeros_like(t)
    for e in range(N_EXPERTS):
        y = (jax.nn.silu(t @ w1[e]) * (t @ w3[e])) @ w2[e]
        out = out + gates[:, e:e + 1] * y
    return out.reshape(b, n, d)


def setup_inputs(seed: int = 0) -> dict:
    key = jax.random.key(seed)
    ks = jax.random.split(key, 32)
    f32 = jnp.float32

    def nrm(k, shape, scale):
        return jax.random.normal(k, shape, f32) * scale

    n_idx = jnp.arange(SSM_STATE, dtype=f32)
    ssm_shape = (DEPTH, N_DIRECTIONS, SSM_GROUPS, SSM_STATE)
    return {
        "x": nrm(ks[0], (BATCH, SEQ, D_MODEL), 1.0),
        "c": nrm(ks[1], (BATCH, D_MODEL), 1.0),
        "ctx": nrm(ks[2], (BATCH, CTX_LEN, D_MODEL), 1.0),
        "c_ctx": nrm(ks[3], (D_MODEL,), 1.0),
        "w_mod": nrm(ks[4], (DEPTH, D_MODEL, N_MOD * D_MODEL), 0.5 * D_MODEL ** -0.5),
        "b_mod": nrm(ks[5], (DEPTH, N_MOD * D_MODEL), 0.02),
        "norm1_g": 1.0 + nrm(ks[6], (DEPTH, D_MODEL), 0.02),
        "norm2_g": 1.0 + nrm(ks[7], (DEPTH, D_MODEL), 0.02),
        "w_in": nrm(ks[8], (DEPTH, D_MODEL, IN_WIDTH), D_MODEL ** -0.5),
        "pool_w": nrm(ks[9], (DEPTH, POOL_GROUPS, POOL_GROUP_DIM, POOL_GROUP_DIM), POOL_GROUP_DIM ** -0.5),
        "pool_scale": 1.0 + nrm(ks[10], (DEPTH, POOL_WIDTH), 0.1),
        "ssm_a_re": -0.5 + nrm(ks[11], ssm_shape, 0.01),
        "ssm_a_im": math.pi * n_idx + nrm(ks[12], ssm_shape, 0.01),
        "ssm_log_dt": jax.random.uniform(ks[13], (DEPTH, N_DIRECTIONS, SSM_GROUPS), f32,
                                         math.log(DT_MIN), math.log(DT_MAX)),
        "ssm_b_re": nrm(ks[14], ssm_shape + (SSM_GROUP_DIM,), (2 * SSM_GROUP_DIM) ** -0.5),
        "ssm_b_im": nrm(ks[15], ssm_shape + (SSM_GROUP_DIM,), (2 * SSM_GROUP_DIM) ** -0.5),
        "ssm_c_re": nrm(ks[16], (DEPTH, N_DIRECTIONS, SSM_GROUPS, SSM_GROUP_DIM, SSM_STATE), SSM_STATE ** -0.5),
        "ssm_c_im": nrm(ks[17], (DEPTH, N_DIRECTIONS, SSM_GROUPS, SSM_GROUP_DIM, SSM_STATE), SSM_STATE ** -0.5),
        "ssm_d": nrm(ks[18], (DEPTH, SSM_WIDTH), 1.0),
        "glu_w": nrm(ks[19], (DEPTH, SSM_WIDTH, SSM_WIDTH), SSM_WIDTH ** -0.5),
        "glu_b": nrm(ks[20], (DEPTH, SSM_WIDTH), 0.02),
        "w_branch_pool": nrm(ks[21], (DEPTH, POOL_WIDTH, D_MODEL), POOL_WIDTH ** -0.5),
        "w_branch_ssm": nrm(ks[22], (DEPTH, SSM_WIDTH, D_MODEL), SSM_WIDTH ** -0.5),
        "w_out": nrm(ks[23], (DEPTH, D_MODEL, D_MODEL), D_MODEL ** -0.5),
        "router_w": nrm(ks[24], (D_MODEL, N_EXPERTS), D_MODEL ** -0.5),
        "router_b": nrm(ks[25], (N_EXPERTS,), 0.01),
        "expert_w1": nrm(ks[26], (DEPTH, N_EXPERTS, D_MODEL, D_EXPERT), D_MODEL ** -0.5),
        "expert_w3": nrm(ks[27], (DEPTH, N_EXPERTS, D_MODEL, D_EXPERT), D_MODEL ** -0.5),
        "expert_w2": nrm(ks[28], (DEPTH, N_EXPERTS, D_EXPERT, D_MODEL), D_EXPERT ** -0.5),
        "final_g": 1.0 + nrm(ks[29], (D_MODEL,), 0.02),
    }


def reference(x, c, ctx, c_ctx, w_mod, b_mod, norm1_g, norm2_g, w_in, pool_w, pool_scale,
              ssm_a_re, ssm_a_im, ssm_log_dt, ssm_b_re, ssm_b_im, ssm_c_re, ssm_c_im, ssm_d,
              glu_w, glu_b, w_branch_pool, w_branch_ssm, w_out, router_w, router_b,
              expert_w1, expert_w3, expert_w2, final_g):
    ctx_s = ctx
    silu_c = jax.nn.silu(c)
    silu_cc = jax.nn.silu(c_ctx)[None, :]
    for i in range(DEPTH):
        last = i == DEPTH - 1
        sh1, sc1, g1, sh2, sc2, g2 = _adaln_params(silu_c, w_mod[i], b_mod[i])
        csh1, csc1, cg1, csh2, csc2, cg2 = _adaln_params(silu_cc, w_mod[i], b_mod[i])
        h_lat = _modulate(_rms_norm(x, norm1_g[i]), sh1, sc1)
        h_ctx = _modulate(_rms_norm(ctx_s, norm1_g[i]), csh1, csc1)
        mix_lat, mix_ctx = _token_mixer(
            h_lat, h_ctx, not last, w_in[i], pool_w[i], pool_scale[i],
            ssm_a_re[i], ssm_a_im[i], ssm_log_dt[i], ssm_b_re[i], ssm_b_im[i],
            ssm_c_re[i], ssm_c_im[i], ssm_d[i], glu_w[i], glu_b[i],
            w_branch_pool[i], w_branch_ssm[i], w_out[i])
        x = x + g1 * mix_lat
        h_lat = _modulate(_rms_norm(x, norm2_g[i]), sh2, sc2)
        x = x + g2 * _moe(h_lat, router_w, router_b, expert_w1[i], expert_w3[i], expert_w2[i])
        if not last:
            ctx_s = ctx_s + cg1 * mix_ctx
            h_ctx = _modulate(_rms_norm(ctx_s, norm2_g[i]), csh2, csc2)
            ctx_s = ctx_s + cg2 * _moe(h_ctx, router_w, router_b, expert_w1[i], expert_w3[i], expert_w2[i])
    return _rms_norm(x, final_g)
```

```python
import functools
import math

import numpy as np
import jax
import jax.numpy as jnp
from jax import lax
from jax.experimental import pallas as pl
from jax.experimental.pallas import tpu as pltpu

F32 = jnp.float32
BF16 = jnp.bfloat16
HIGHEST = lax.Precision.HIGHEST

GRID_W = 64
POOL_WINDOWS = (2, 4, 8, 16)
POOL_GROUP_DIM = 128
POOL_WIDTH = 512
SSM_WIDTH = 512
SSM_GROUP_DIM = 16
SSM_GROUPS = 32
SSM_STATE = 64
N_MOD = 6
N_EXPERTS = 16
EXPERTS_PER_GROUP = 4
N_EXPERT_GROUPS = 4
RMS_EPS = 1e-6

SSM_CHUNK = 16
SSM_PAIRS = SSM_GROUPS // 2
POOL_TILE = 256
LANES = 128
SUBLANES = 8
V7X_VMEM_LIMIT = 56 * 1024 * 1024
MOD_ROWS = 8


def _cparams(*sem):
    return pltpu.CompilerParams(dimension_semantics=sem, vmem_limit_bytes=V7X_VMEM_LIMIT)


def _pow2_tile(limit, *sizes):
    t = limit
    while any(s % t for s in sizes):
        t //= 2
    return t


def _sigmoid(v):
    return 1.0 / (1.0 + jnp.exp(-v))


def _silu(v):
    return v * _sigmoid(v)


def _gelu_tanh(v):
    c = math.sqrt(2.0 / math.pi)
    return 0.5 * v * (1.0 + jnp.tanh(c * (v + 0.044715 * (v * v * v))))


def _rms(x, g):
    ms = jnp.mean(x * x, axis=-1, keepdims=True)
    return x * lax.rsqrt(ms + RMS_EPS) * g


def _bdot(a, b):
    return jnp.dot(a, b, preferred_element_type=F32)


def _mod_kernel(c_ref, w_ref, b_ref, o_ref):
    s = _silu(c_ref[...])
    o_ref[...] = jnp.dot(s, w_ref[...], preferred_element_type=F32, precision=HIGHEST) + b_ref[...]


def _adaln_tables(c_rows, w_mod, b_mod):
    depth, d, _ = w_mod.shape
    nrow = c_rows.shape[0]
    out = pl.pallas_call(
        _mod_kernel,
        grid=(depth, N_MOD),
        in_specs=[
            pl.BlockSpec((nrow, d), lambda l, j: (0, 0)),
            pl.BlockSpec((None, d, d), lambda l, j: (l, 0, j)),
            pl.BlockSpec((None, None, 1, d), lambda l, j: (l, j, 0, 0)),
        ],
        out_specs=pl.BlockSpec((None, None, nrow, d), lambda l, j: (l, j, 0, 0)),
        out_shape=jax.ShapeDtypeStruct((depth, N_MOD, nrow, d), F32),
        compiler_params=_cparams("parallel", "parallel"),
        name="adaln_tables",
    )(c_rows, w_mod, b_mod.reshape(depth, N_MOD, 1, d))
    out = jnp.transpose(out, (0, 2, 1, 3))
    return jnp.pad(out, ((0, 0), (0, 0), (0, MOD_ROWS - N_MOD), (0, 0)))


def _inproj_kernel(x_ref, mod_ref, g_ref, w_ref, up_ref, us_ref, gp_ref, gs_ref):
    h = _rms(x_ref[...], g_ref[...]) * (1.0 + mod_ref[1:2, :]) + mod_ref[0:1, :]
    hb = h.astype(BF16)
    d = gp_ref.shape[1]
    o0, o1, o2 = POOL_WIDTH, POOL_WIDTH + SSM_WIDTH, POOL_WIDTH + SSM_WIDTH + d
    up_ref[...] = _bdot(hb, w_ref[:, 0:o0])
    us_ref[...] = _bdot(hb, w_ref[:, o0:o1]).astype(BF16)
    gp_ref[...] = _bdot(hb, w_ref[:, o1:o2]).astype(BF16)
    gs_ref[...] = _bdot(hb, w_ref[:, o2:]).astype(BF16)


def _mod_index(i, lat_tiles, tiles_per_batch, n_batch):
    return jnp.where(i < lat_tiles, i // tiles_per_batch, n_batch)


def _inproj(x_all, mod_l, g1, w_in_bf, geo):
    rows, d = x_all.shape
    tm = geo["tm"]
    mod_map = lambda i: (_mod_index(i, geo["t_lat"] // tm, geo["seq"] // tm, geo["batch"]), 0, 0)
    row_map = lambda i: (i, 0)
    return pl.pallas_call(
        _inproj_kernel,
        grid=(rows // tm,),
        in_specs=[
            pl.BlockSpec((tm, d), row_map),
            pl.BlockSpec((None, MOD_ROWS, d), mod_map),
            pl.BlockSpec((1, d), lambda i: (0, 0)),
            pl.BlockSpec(w_in_bf.shape, lambda i: (0, 0)),
        ],
        out_specs=[
            pl.BlockSpec((tm, POOL_WIDTH), row_map),
            pl.BlockSpec((tm, SSM_WIDTH), row_map),
            pl.BlockSpec((tm, d), row_map),
            pl.BlockSpec((tm, d), row_map),
        ],
        out_shape=[
            jax.ShapeDtypeStruct((rows, POOL_WIDTH), F32),
            jax.ShapeDtypeStruct((rows, SSM_WIDTH), BF16),
            jax.ShapeDtypeStruct((rows, d), BF16),
            jax.ShapeDtypeStruct((rows, d), BF16),
        ],
        compiler_params=_cparams("parallel"),
        name="inproj",
    )(x_all, mod_l, g1, w_in_bf)


def _box_count(pos, w, n):
    return jnp.minimum(pos + w // 2, n) - jnp.maximum(pos - w // 2, 0)


def _split_bf16(v):
    hi = v.astype(BF16)
    lo = (v - hi.astype(F32)).astype(BF16)
    return hi, lo


def _pool_kernel(u_ref, a_ref, pw_ref, ps_ref, *rest, n_tok, grid_rows):
    if grid_rows is None:
        (o_ref,) = rest
    else:
        _, o_ref, pad_ref = rest
    gi = pl.program_id(1)
    tile = min(POOL_TILE, n_tok)
    n_tiles = n_tok // tile
    halo = (max(POOL_WINDOWS) // 2) * GRID_W

    for widx, w in enumerate(POOL_WINDOWS):

        @pl.when(gi == widx)
        def _(w=w):
            a = a_ref[...]

            def colsum(k):
                v = u_ref[pl.ds(k * tile, tile), :]
                hi, lo = _split_bf16(v)
                return v, _bdot(a, hi) + _bdot(a, lo)

            def finish(k, v, total, cnt):
                p = (total / cnt.astype(F32) - v).astype(BF16)
                y = _bdot(p, pw_ref[...]) * ps_ref[...]
                o_ref[pl.ds(k * tile, tile), :] = y.astype(o_ref.dtype)

            if grid_rows is None:
                for k in range(n_tiles):
                    v, total = colsum(k)
                    pos = lax.broadcasted_iota(jnp.int32, (tile, POOL_GROUP_DIM), 0) + k * tile
                    finish(k, v, total, _box_count(pos, w, n_tok))
            else:
                zeros = jnp.zeros((halo, POOL_GROUP_DIM), F32)
                pad_ref[pl.ds(0, halo), :] = zeros
                pad_ref[pl.ds(halo + n_tok, halo), :] = zeros
                for k in range(n_tiles):
                    _, total = colsum(k)
                    pad_ref[pl.ds(halo + k * tile, tile), :] = total
                for k in range(n_tiles):
                    acc = None
                    for sh in range(-(w // 2), w // 2):
                        part = pad_ref[pl.ds(halo + k * tile + sh * GRID_W, tile), :]
                        acc = part if acc is None else acc + part
                    tok = lax.broadcasted_iota(jnp.int32, (tile, POOL_GROUP_DIM), 0) + k * tile
                    cnt = (_box_count(tok % GRID_W, w, GRID_W)
                           * _box_count(tok // GRID_W, w, grid_rows))
                    finish(k, u_ref[pl.ds(k * tile, tile), :], acc, cnt)


def _window_matrix(w, n, period):
    t = np.arange(n)[:, None]
    s = np.arange(n)[None, :]
    inside = (s >= t - w // 2) & (s < t - w // 2 + w) & (t // period == s // period)
    return inside.astype(np.float32)


def _pool_branch(u_pool, pool_w_bf, pool_scale, geo):
    rows = u_pool.shape[0]
    batch, seq, ctx_len, t_lat = geo["batch"], geo["seq"], geo["ctx"], geo["t_lat"]
    n_groups = len(POOL_WINDOWS)
    ps = pool_scale.reshape(n_groups, 1, POOL_GROUP_DIM)
    tile = min(POOL_TILE, seq)
    a_lat = jnp.asarray(np.stack([_window_matrix(w, tile, GRID_W) for w in POOL_WINDOWS]), BF16)
    a_ctx = jnp.asarray(np.stack([_window_matrix(w, ctx_len, ctx_len) for w in POOL_WINDOWS]), BF16)
    halo = (max(POOL_WINDOWS) // 2) * GRID_W
    out_shape = jax.ShapeDtypeStruct((rows, POOL_WIDTH), BF16)

    def specs(n_tok, row0, a_dim):
        blk = pl.BlockSpec((n_tok, POOL_GROUP_DIM), lambda b, g: (row0 + b, g))
        return blk, [
            blk,
            pl.BlockSpec((None, a_dim, a_dim), lambda b, g: (g, 0, 0)),
            pl.BlockSpec((None, POOL_GROUP_DIM, POOL_GROUP_DIM), lambda b, g: (g, 0, 0)),
            pl.BlockSpec((None, 1, POOL_GROUP_DIM), lambda b, g: (g, 0, 0)),
        ]

    blk, in_specs = specs(ctx_len, t_lat // ctx_len, ctx_len)
    p_ctx = pl.pallas_call(
        functools.partial(_pool_kernel, n_tok=ctx_len, grid_rows=None),
        grid=(batch, n_groups),
        in_specs=in_specs,
        out_specs=blk,
        out_shape=out_shape,
        compiler_params=_cparams("parallel", "parallel"),
        name="pool_ctx",
    )(u_pool, a_ctx, pool_w_bf, ps)

    blk, in_specs = specs(seq, 0, tile)
    return pl.pallas_call(
        functools.partial(_pool_kernel, n_tok=seq, grid_rows=seq // GRID_W),
        grid=(batch, n_groups),
        in_specs=in_specs + [pl.BlockSpec(memory_space=pl.ANY)],
        out_specs=blk,
        out_shape=out_shape,
        scratch_shapes=[pltpu.VMEM((seq + 2 * halo, POOL_GROUP_DIM), F32)],
        input_output_aliases={4: 0},
        compiler_params=_cparams("parallel", "parallel"),
        name="pool_lat",
    )(u_pool, a_lat, pool_w_bf, ps, p_ctx)


def _ssm_operators(a_re, a_im, log_dt, b_re, b_im, c_re, c_im, d_skip):
    ng, ns, nj, lc = SSM_GROUPS, SSM_STATE, SSM_GROUP_DIM, SSM_CHUNK
    dt = jnp.exp(log_dt)[..., None]
    mag = jnp.exp(a_re * dt)
    abar_re = mag * jnp.cos(a_im * dt)
    abar_im = mag * jnp.sin(a_im * dt)
    nr, ni = abar_re - 1.0, abar_im
    den = a_re * a_re + a_im * a_im
    f_re = (nr * a_re + ni * a_im) / den
    f_im = (ni * a_re - nr * a_im) / den
    bbar_re = f_re[..., None] * b_re - f_im[..., None] * b_im
    bbar_im = f_re[..., None] * b_im + f_im[..., None] * b_re

    pr, pi = [jnp.ones_like(abar_re)], [jnp.zeros_like(abar_im)]
    for _ in range(lc):
        pr.append(pr[-1] * abar_re - pi[-1] * abar_im)
        pi.append(pr[-2] * abar_im + pi[-1] * abar_re)
    ak_re, ak_im = jnp.stack(pr), jnp.stack(pi)

    ab_re = ak_re[:lc, ..., None] * bbar_re - ak_im[:lc, ..., None] * bbar_im
    ab_im = ak_re[:lc, ..., None] * bbar_im + ak_im[:lc, ..., None] * bbar_re
    kern = (jnp.einsum("dgop,kdgpi->dkgoi", c_re, ab_re, precision=HIGHEST)
            - jnp.einsum("dgop,kdgpi->dkgoi", c_im, ab_im, precision=HIGHEST))

    s_idx = np.arange(lc)[:, None]
    t_idx = np.arange(lc)[None, :]
    lag_f = np.clip(t_idx - s_idx, 0, lc - 1)
    lag_b = np.clip(s_idx - t_idx, 0, lc - 1)
    mask_f = jnp.asarray((t_idx >= s_idx), F32)[:, :, None, None, None]
    mask_b = jnp.asarray((s_idx >= t_idx), F32)[:, :, None, None, None]
    m5 = kern[0][lag_f] * mask_f + kern[1][lag_b] * mask_b
    m5 = jnp.transpose(m5, (2, 0, 4, 1, 3))
    eye = jnp.asarray(np.eye(lc)[:, None, :, None] * np.eye(nj)[None, :, None, :], F32)
    m5 = m5 + eye[None] * d_skip.reshape(ng, 1, nj, 1, 1)
    m_mat = m5.reshape(ng, lc * nj, lc * nj)

    pair_eye = jnp.asarray(np.eye(2), F32)

    def pair_rows_to_cols(w):
        g2 = w.reshape(SSM_PAIRS, 2, w.shape[1], w.shape[2])
        out = jnp.einsum("qarc,ab->qarbc", g2, pair_eye)
        return out.reshape(SSM_PAIRS, 2 * w.shape[1], 2 * w.shape[2])

    def st(ab, d, order):
        sel = ab[order, d]
        return jnp.transpose(sel, (1, 0, 3, 2)).reshape(ng, lc * nj, ns)

    fwd_order = np.arange(lc)[::-1]
    bwd_order = np.arange(lc)
    w_st = jnp.concatenate([
        pair_rows_to_cols(st(ab_re, 0, fwd_order)), pair_rows_to_cols(st(ab_im, 0, fwd_order)),
        pair_rows_to_cols(st(ab_re, 1, bwd_order)), pair_rows_to_cols(st(ab_im, 1, bwd_order)),
    ], axis=2)

    def so(d, order):
        ar = ak_re[order, d][:, :, None, :]
        ai = ak_im[order, d][:, :, None, :]
        re = c_re[d][None] * ar - c_im[d][None] * ai
        im = c_re[d][None] * ai + c_im[d][None] * ar
        to_mat = lambda v: jnp.transpose(v, (1, 3, 0, 2)).reshape(ng, ns, lc * nj)
        return to_mat(re), to_mat(-im)

    of_re, of_im = so(0, np.arange(1, lc + 1))
    ob_re, ob_im = so(1, np.arange(lc, 0, -1))
    w_out = jnp.concatenate([pair_rows_to_cols(of_re), pair_rows_to_cols(of_im),
                             pair_rows_to_cols(ob_re), pair_rows_to_cols(ob_im)], axis=1)

    def pair_vec(v):
        return v.reshape(SSM_PAIRS, 2 * ns)

    a_pow = jnp.stack([pair_vec(ak_re[lc, 0]), pair_vec(ak_im[lc, 0]),
                       pair_vec(ak_re[lc, 1]), pair_vec(ak_im[lc, 1])], axis=1)
    a_pow = jnp.pad(a_pow, ((0, 0), (0, SUBLANES - 4), (0, 0)))
    return m_mat.astype(BF16), w_st.astype(BF16), w_out.astype(BF16), a_pow


def _ssm_state_kernel(u_ref, w_ref, s_ref):
    s_ref[...] = _bdot(u_ref[...], w_ref[...])


def _ssm_scan_kernel(s_ref, a_ref, h_ref, *, n_chunks, n_ctx_chunks, batch):
    shape = (batch, LANES)
    afr = jnp.broadcast_to(a_ref[0:1, :], shape)
    afi = jnp.broadcast_to(a_ref[1:2, :], shape)
    abr = jnp.broadcast_to(a_ref[2:3, :], shape)
    abi = jnp.broadcast_to(a_ref[3:4, :], shape)
    zero = jnp.zeros(shape, F32)

    def step(k, carry):
        fr, fi, br, bi = carry
        rf = pl.multiple_of(k * batch, batch)
        nb = jnp.where(k < n_ctx_chunks, n_ctx_chunks - 1 - k, n_chunks - 1 - k + n_ctx_chunks)
        rb = pl.multiple_of(nb * batch, batch)
        h_ref[pl.ds(rf, batch), 0:LANES] = fr
        h_ref[pl.ds(rf, batch), LANES:2 * LANES] = fi
        h_ref[pl.ds(rb, batch), 2 * LANES:3 * LANES] = br
        h_ref[pl.ds(rb, batch), 3 * LANES:4 * LANES] = bi
        xr = s_ref[pl.ds(rf, batch), 0:LANES]
        xi = s_ref[pl.ds(rf, batch), LANES:2 * LANES]
        yr = s_ref[pl.ds(rb, batch), 2 * LANES:3 * LANES]
        yi = s_ref[pl.ds(rb, batch), 3 * LANES:4 * LANES]
        return (fr * afr - fi * afi + xr, fi * afr + fr * afi + xi,
                br * abr - bi * abi + yr, bi * abr + br * abi + yi)

    lax.fori_loop(0, n_chunks, step, (zero, zero, zero, zero))


def _ssm_out_kernel(u_ref, h_ref, m_ref, w_ref, y_ref):
    half = SSM_CHUNK * SSM_GROUP_DIM
    u = u_ref[...]
    carry = _bdot(h_ref[...].astype(BF16), w_ref[...])
    y0 = _bdot(u[:, :half], m_ref[0]) + carry[:, :half]
    y1 = _bdot(u[:, half:], m_ref[1]) + carry[:, half:]
    y_ref[:, :half] = y0.astype(y_ref.dtype)
    y_ref[:, half:] = y1.astype(y_ref.dtype)


def _ssm_branch(u_ssm, ops, geo):
    m_mat, w_st, w_out, a_pow = ops
    batch, seq, ctx_len, t_lat = geo["batch"], geo["seq"], geo["ctx"], geo["t_lat"]
    lc, nj = SSM_CHUNK, SSM_GROUP_DIM
    ncl, ncc = seq // lc, ctx_len // lc
    nc = ncl + ncc
    ncb = nc * batch
    width = 2 * lc * nj

    def to_chunks(v, n):
        return v.reshape(batch, n, lc, SSM_PAIRS, 2, nj)

    u6 = jnp.concatenate([to_chunks(u_ssm[t_lat:], ncc), to_chunks(u_ssm[:t_lat], ncl)], axis=1)
    u_p = jnp.transpose(u6, (3, 1, 0, 4, 2, 5)).reshape(SSM_PAIRS, ncb, width)

    pair_blk = pl.BlockSpec((None, ncb, width), lambda q: (q, 0, 0))
    col_blk = pl.BlockSpec((ncb, width), lambda q: (0, q))
    wmat_blk = pl.BlockSpec((None, width, width), lambda q: (q, 0, 0))

    s_all = pl.pallas_call(
        _ssm_state_kernel,
        grid=(SSM_PAIRS,),
        in_specs=[pair_blk, wmat_blk],
        out_specs=col_blk,
        out_shape=jax.ShapeDtypeStruct((ncb, SSM_PAIRS * width), F32),
        compiler_params=_cparams("parallel"),
        name="ssm_state",
    )(u_p, w_st)

    h_all = pl.pallas_call(
        functools.partial(_ssm_scan_kernel, n_chunks=nc, n_ctx_chunks=ncc, batch=batch),
        grid=(SSM_PAIRS,),
        in_specs=[col_blk, pl.BlockSpec((None, SUBLANES, LANES), lambda q: (q, 0, 0))],
        out_specs=col_blk,
        out_shape=jax.ShapeDtypeStruct((ncb, SSM_PAIRS * width), F32),
        compiler_params=_cparams("parallel"),
        name="ssm_scan",
    )(s_all, a_pow)

    y_p = pl.pallas_call(
        _ssm_out_kernel,
        grid=(SSM_PAIRS,),
        in_specs=[pair_blk, col_blk,
                  pl.BlockSpec((2, lc * nj, lc * nj), lambda q: (q, 0, 0)), wmat_blk],
        out_specs=pair_blk,
        out_shape=jax.ShapeDtypeStruct((SSM_PAIRS, ncb, width), BF16),
        compiler_params=_cparams("parallel"),
        name="ssm_out",
    )(u_p, h_all, m_mat, w_out)

    y6 = jnp.transpose(y_p.reshape(SSM_PAIRS, nc, batch, 2, lc, nj), (2, 1, 4, 0, 3, 5))
    y_ctx = y6[:, :ncc].reshape(batch * ctx_len, SSM_WIDTH)
    y_lat = y6[:, ncc:].reshape(t_lat, SSM_WIDTH)
    return jnp.concatenate([y_lat, y_ctx], axis=0)


def _route(logits, rb):
    scores = _sigmoid(logits)
    biased = scores + rb
    sc = [scores[:, e:e + 1] for e in range(N_EXPERTS)]
    bc = [biased[:, e:e + 1] for e in range(N_EXPERTS)]
    npg = EXPERTS_PER_GROUP

    group_scores = []
    for g in range(N_EXPERT_GROUPS):
        v = bc[g * npg:(g + 1) * npg]
        best = None
        for i in range(npg):
            for j in range(i + 1, npg):
                pair = v[i] + v[j]
                best = pair if best is None else jnp.maximum(best, pair)
        group_scores.append(best)

    bg = jnp.zeros_like(group_scores[0], dtype=jnp.int32)
    best = group_scores[0]
    for g in range(1, N_EXPERT_GROUPS):
        upd = group_scores[g] > best
        bg = jnp.where(upd, g, bg)
        best = jnp.where(upd, group_scores[g], best)

    vb = list(bc[:npg])
    vs = list(sc[:npg])
    for g in range(1, N_EXPERT_GROUPS):
        sel = bg == g
        for j in range(npg):
            vb[j] = jnp.where(sel, bc[g * npg + j], vb[j])
            vs[j] = jnp.where(sel, sc[g * npg + j], vs[j])

    i1 = jnp.zeros_like(bg)
    m1, w1 = vb[0], vs[0]
    for j in range(1, npg):
        upd = vb[j] > m1
        i1 = jnp.where(upd, j, i1)
        m1 = jnp.where(upd, vb[j], m1)
        w1 = jnp.where(upd, vs[j], w1)

    neg = jnp.full_like(m1, -jnp.inf)
    i2 = jnp.zeros_like(bg)
    m2, w2 = neg, jnp.zeros_like(w1)
    for j in range(npg):
        cand = jnp.where(i1 == j, neg, vb[j])
        upd = cand > m2
        i2 = jnp.where(upd, j, i2)
        m2 = jnp.where(upd, cand, m2)
        w2 = jnp.where(upd, vs[j], w2)

    den = w1 + w2
    e1 = bg * npg + i1
    e2 = bg * npg + i2
    lane = lax.broadcasted_iota(jnp.int32, logits.shape, 1)
    return (jnp.where(lane == e1, w1 / den, 0.0) + jnp.where(lane == e2, w2 / den, 0.0))


def _merge_kernel(x_ref, y_ref, p_ref, gp_ref, gs_ref, mod_ref, g2_ref, gluw_ref, glub_ref,
                  wbp_ref, wbs_ref, wo_ref, rw_ref, rb_ref, x1_ref, h2_ref, gate_ref):
    s = _gelu_tanh(y_ref[...].astype(F32))
    s = s * _sigmoid(_bdot(s.astype(BF16), gluw_ref[...]) + glub_ref[...])
    m = (_sigmoid(gp_ref[...].astype(F32)) * _bdot(p_ref[...], wbp_ref[...])
         + _sigmoid(gs_ref[...].astype(F32)) * _bdot(s.astype(BF16), wbs_ref[...]))
    x1 = x_ref[...] + mod_ref[2:3, :] * _bdot(m.astype(BF16), wo_ref[...])
    x1_ref[...] = x1
    h2 = _rms(x1, g2_ref[...]) * (1.0 + mod_ref[4:5, :]) + mod_ref[3:4, :]
    h2_ref[...] = h2.astype(BF16)
    logits = jnp.dot(h2, rw_ref[...], preferred_element_type=F32, precision=HIGHEST)
    gate_ref[...] = _route(logits, rb_ref[...])


def _merge(x_all, y_ssm, p_pool, g_pool, g_ssm, mod_l, g2, wts, router, n_rows, geo):
    d = x_all.shape[1]
    tm = geo["tm"]
    mod_map = lambda i: (_mod_index(i, geo["t_lat"] // tm, geo["seq"] // tm, geo["batch"]), 0, 0)
    row_map = lambda i: (i, 0)
    full = lambda a: pl.BlockSpec(a.shape, lambda i: (0,) * a.ndim)
    glu_w, glu_b, w_bp, w_bs, w_o = wts
    rw, rb = router
    return pl.pallas_call(
        _merge_kernel,
        grid=(n_rows // tm,),
        in_specs=[
            pl.BlockSpec((tm, d), row_map),
            pl.BlockSpec((tm, SSM_WIDTH), row_map),
            pl.BlockSpec((tm, POOL_WIDTH), row_map),
            pl.BlockSpec((tm, d), row_map),
            pl.BlockSpec((tm, d), row_map),
            pl.BlockSpec((None, MOD_ROWS, d), mod_map),
            full(g2), full(glu_w), full(glu_b), full(w_bp), full(w_bs), full(w_o), full(rw), full(rb),
        ],
        out_specs=[
            pl.BlockSpec((tm, d), row_map),
            pl.BlockSpec((tm, d), row_map),
            pl.BlockSpec((tm, LANES), row_map),
        ],
        out_shape=[
            jax.ShapeDtypeStruct((n_rows, d), F32),
            jax.ShapeDtypeStruct((n_rows, d), BF16),
            jax.ShapeDtypeStruct((n_rows, LANES), F32),
        ],
        compiler_params=_cparams("parallel"),
        name="merge_router",
    )(x_all, y_ssm, p_pool, g_pool, g_ssm, mod_l, g2, glu_w, glu_b, w_bp, w_bs, w_o, rw, rb)


def _moe_kernel(h_ref, gate_ref, x1_ref, mod_ref, fg_ref, w1_ref, w3_ref, w2_ref, o_ref, acc_ref,
                *, final_norm):
    e = pl.program_id(1)

    @pl.when(e == 0)
    def _():
        acc_ref[...] = jnp.zeros_like(acc_ref)

    h = h_ref[...]
    hid = (_silu(_bdot(h, w1_ref[...])) * _bdot(h, w3_ref[...])).astype(BF16)
    y = _bdot(hid, w2_ref[...])
    gates = gate_ref[...]
    lane = lax.broadcasted_iota(jnp.int32, gates.shape, 1)
    ge = jnp.sum(jnp.where(lane == e, gates, 0.0), axis=1, keepdims=True)
    acc_ref[...] += ge * y

    @pl.when(e == pl.num_programs(1) - 1)
    def _():
        x2 = x1_ref[...] + mod_ref[5:6, :] * acc_ref[...]
        o_ref[...] = _rms(x2, fg_ref[...]) if final_norm else x2


def _moe(h2, gates, x1, mod_l, final_g, w1, w3, w2, final_norm, geo):
    n_rows, d = x1.shape
    tm = geo["tm_moe"]
    n_exp, _, d_exp = w1.shape
    mod_map = lambda i, e: (_mod_index(i, geo["t_lat"] // tm, geo["seq"] // tm, geo["batch"]), 0, 0)
    row_map = lambda i, e: (i, 0)
    return pl.pallas_call(
        functools.partial(_moe_kernel, final_norm=final_norm),
        grid=(n_rows // tm, n_exp),
        in_specs=[
            pl.BlockSpec((tm, d), row_map),
            pl.BlockSpec((tm, LANES), row_map),
            pl.BlockSpec((tm, d), row_map),
            pl.BlockSpec((None, MOD_ROWS, d), mod_map),
            pl.BlockSpec((1, d), lambda i, e: (0, 0)),
            pl.BlockSpec((None, d, d_exp), lambda i, e: (e, 0, 0)),
            pl.BlockSpec((None, d, d_exp), lambda i, e: (e, 0, 0)),
            pl.BlockSpec((None, d_exp, d), lambda i, e: (e, 0, 0)),
        ],
        out_specs=pl.BlockSpec((tm, d), row_map),
        out_shape=jax.ShapeDtypeStruct((n_rows, d), F32),
        scratch_shapes=[pltpu.VMEM((tm, d), F32)],
        compiler_params=_cparams("parallel", "arbitrary"),
        name="moe_dense",
    )(h2, gates, x1, mod_l, final_g, w1, w3, w2)


def kernel(x, c, ctx, c_ctx, w_mod, b_mod, norm1_g, norm2_g, w_in, pool_w, pool_scale, ssm_a_re, ssm_a_im, ssm_log_dt, ssm_b_re, ssm_b_im, ssm_c_re, ssm_c_im, ssm_d, glu_w, glu_b, w_branch_pool, w_branch_ssm, w_out, router_w, router_b, expert_w1, expert_w3, expert_w2, final_g):
    batch, seq, d = x.shape
    ctx_len = ctx.shape[1]
    depth = w_mod.shape[0]
    t_lat, t_ctx = batch * seq, batch * ctx_len
    assert seq % GRID_W == 0 and seq % POOL_TILE == 0 and ctx_len % SSM_CHUNK == 0
    assert t_lat % ctx_len == 0 and batch == SUBLANES
    geo = dict(batch=batch, seq=seq, ctx=ctx_len, t_lat=t_lat,
               tm=_pow2_tile(512, seq, t_ctx), tm_moe=_pow2_tile(1024, seq, t_ctx))

    n_cond = 2 * SUBLANES
    c_rows = jnp.concatenate([c, c_ctx[None, :], jnp.zeros((n_cond - batch - 1, d), F32)], axis=0)
    mod = _adaln_tables(c_rows, w_mod, b_mod)

    rw = jnp.pad(router_w, ((0, 0), (0, LANES - N_EXPERTS)))
    rb = jnp.pad(router_b, (0, LANES - N_EXPERTS)).reshape(1, LANES)
    fg = final_g.reshape(1, d)

    x_all = jnp.concatenate([x.reshape(t_lat, d), ctx.reshape(t_ctx, d)], axis=0)
    for i in range(depth):
        last = i == depth - 1
        n_rows = t_lat if last else t_lat + t_ctx
        ops = _ssm_operators(ssm_a_re[i], ssm_a_im[i], ssm_log_dt[i], ssm_b_re[i], ssm_b_im[i],
                             ssm_c_re[i], ssm_c_im[i], ssm_d[i])
        u_pool, u_ssm, g_pool, g_ssm = _inproj(
            x_all, mod[i], norm1_g[i].reshape(1, d), w_in[i].astype(BF16), geo)
        p_pool = _pool_branch(u_pool, pool_w[i].astype(BF16), pool_scale[i], geo)
        y_ssm = _ssm_branch(u_ssm, ops, geo)
        wts = (glu_w[i].astype(BF16), glu_b[i].reshape(1, SSM_WIDTH), w_branch_pool[i].astype(BF16),
               w_branch_ssm[i].astype(BF16), w_out[i].astype(BF16))
        x1, h2, gates = _merge(x_all, y_ssm, p_pool, g_pool, g_ssm, mod[i],
                               norm2_g[i].reshape(1, d), wts, (rw, rb), n_rows, geo)
        x_all = _moe(h2, gates, x1, mod[i], fg, expert_w1[i].astype(BF16),
                     expert_w3[i].astype(BF16), expert_w2[i].astype(BF16), last, geo)
    return x_all.reshape(batch, seq, d)
```

```python
import functools
import math

import numpy as np
import jax
import jax.numpy as jnp
from jax import lax
from jax.experimental import pallas as pl
from jax.experimental.pallas import tpu as pltpu

F32 = jnp.float32
BF16 = jnp.bfloat16
HIGHEST = lax.Precision.HIGHEST

GRID_W = 64
POOL_WINDOWS = (2, 4, 8, 16)
POOL_GROUP_DIM = 128
POOL_WIDTH = 512
SSM_WIDTH = 512
SSM_GROUP_DIM = 16
SSM_GROUPS = 32
SSM_STATE = 64
N_MOD = 6
N_EXPERTS = 16
EXPERTS_PER_GROUP = 4
N_EXPERT_GROUPS = 4
RMS_EPS = 1e-6

SSM_CHUNK = 16
SSM_OCTETS = SSM_GROUPS // 8
SSM_SLABS = SSM_GROUPS // 2
POOL_TILE = 256
LANES = 128
SUBLANES = 8
V7X_VMEM_LIMIT = 56 * 1024 * 1024
MOD_ROWS = 8


def _cparams(*sem):
    return pltpu.CompilerParams(dimension_semantics=sem, vmem_limit_bytes=V7X_VMEM_LIMIT)


def _pow2_tile(limit, *sizes):
    t = limit
    while any(s % t for s in sizes):
        t //= 2
    return t


def _sigmoid(v):
    return 0.5 * jnp.tanh(0.5 * v) + 0.5


def _silu(v):
    return v * _sigmoid(v)


def _gelu_tanh(v):
    c = math.sqrt(2.0 / math.pi)
    return 0.5 * v * (1.0 + jnp.tanh(c * (v + 0.044715 * (v * v * v))))


def _rms(x, g):
    ms = jnp.mean(x * x, axis=-1, keepdims=True)
    return x * lax.rsqrt(ms + RMS_EPS) * g


def _bdot(a, b):
    return jnp.dot(a, b, preferred_element_type=F32)


def _mod_kernel(c_ref, w_ref, b_ref, o_ref):
    s = _silu(c_ref[...])
    o_ref[...] = jnp.dot(s, w_ref[...], preferred_element_type=F32, precision=HIGHEST) + b_ref[...]


def _adaln_tables(c_rows, w_mod, b_mod):
    depth, d, _ = w_mod.shape
    nrow = c_rows.shape[0]
    out = pl.pallas_call(
        _mod_kernel,
        grid=(depth, N_MOD),
        in_specs=[
            pl.BlockSpec((nrow, d), lambda l, j: (0, 0)),
            pl.BlockSpec((None, d, d), lambda l, j: (l, 0, j)),
            pl.BlockSpec((None, None, 1, d), lambda l, j: (l, j, 0, 0)),
        ],
        out_specs=pl.BlockSpec((None, None, nrow, d), lambda l, j: (l, j, 0, 0)),
        out_shape=jax.ShapeDtypeStruct((depth, N_MOD, nrow, d), F32),
        compiler_params=_cparams("parallel", "parallel"),
        name="adaln_tables",
    )(c_rows, w_mod, b_mod.reshape(depth, N_MOD, 1, d))
    out = jnp.transpose(out, (0, 2, 1, 3))
    return jnp.pad(out, ((0, 0), (0, 0), (0, MOD_ROWS - N_MOD), (0, 0)))


def _inproj_kernel(x_ref, mod_ref, g_ref, w_ref, up_ref, us_ref, gp_ref, gs_ref, zs_ref):
    h = _rms(x_ref[...], g_ref[...]) * (1.0 + mod_ref[1:2, :]) + mod_ref[0:1, :]
    hb = h.astype(BF16)
    d = gp_ref.shape[1]
    o0, o1, o2 = POOL_WIDTH, POOL_WIDTH + SSM_WIDTH, POOL_WIDTH + SSM_WIDTH + d
    up_ref[...] = _bdot(hb, w_ref[:, 0:o0])
    gp_ref[...] = _bdot(hb, w_ref[:, o1:o2]).astype(BF16)
    gs_ref[...] = _bdot(hb, w_ref[:, o2:]).astype(BF16)
    zs = _bdot(hb, w_ref[:, o0:o1])
    n_chunks = zs_ref.shape[1] // SSM_CHUNK
    for j in range(zs_ref.shape[0]):
        zs_ref[j] = zs[:, j * LANES:(j + 1) * LANES]
    for t in range(SSM_CHUNK):
        for j in range(zs_ref.shape[0]):
            piece = zs_ref[j, pl.ds(t, n_chunks, stride=SSM_CHUNK), :]
            us_ref[t, :, j * LANES:(j + 1) * LANES] = piece.astype(BF16)


def _mod_index(i, lat_tiles, tiles_per_batch, n_batch):
    return jnp.where(i < lat_tiles, i // tiles_per_batch, n_batch)


def _inproj(x_all, mod_l, g1, w_in_bf, geo):
    rows, d = x_all.shape
    tm = geo["tm"]
    mod_map = lambda i: (_mod_index(i, geo["t_lat"] // tm, geo["seq"] // tm, geo["batch"]), 0, 0)
    row_map = lambda i: (i, 0)
    return pl.pallas_call(
        _inproj_kernel,
        grid=(rows // tm,),
        in_specs=[
            pl.BlockSpec((tm, d), row_map),
            pl.BlockSpec((None, MOD_ROWS, d), mod_map),
            pl.BlockSpec((1, d), lambda i: (0, 0)),
            pl.BlockSpec(w_in_bf.shape, lambda i: (0, 0)),
        ],
        out_specs=[
            pl.BlockSpec((tm, POOL_WIDTH), row_map),
            pl.BlockSpec((SSM_CHUNK, tm // SSM_CHUNK, SSM_WIDTH), lambda i: (0, i, 0)),
            pl.BlockSpec((tm, d), row_map),
            pl.BlockSpec((tm, d), row_map),
        ],
        out_shape=[
            jax.ShapeDtypeStruct((rows, POOL_WIDTH), F32),
            jax.ShapeDtypeStruct((SSM_CHUNK, rows // SSM_CHUNK, SSM_WIDTH), BF16),
            jax.ShapeDtypeStruct((rows, d), BF16),
            jax.ShapeDtypeStruct((rows, d), BF16),
        ],
        scratch_shapes=[pltpu.VMEM((SSM_WIDTH // LANES, tm, LANES), F32)],
        compiler_params=_cparams("parallel"),
        name="inproj",
    )(x_all, mod_l, g1, w_in_bf)


def _box_count(pos, w, n):
    return jnp.minimum(pos + w // 2, n) - jnp.maximum(pos - w // 2, 0)


def _split_bf16(v):
    hi = v.astype(BF16)
    lo = (v - hi.astype(F32)).astype(BF16)
    return hi, lo


def _pool_kernel(u_ref, a_ref, pw_ref, ps_ref, *rest, n_tok, grid_rows):
    o_ref = rest[0]
    pad_ref = rest[1] if grid_rows is not None else None
    gi = pl.program_id(1)
    tile = min(POOL_TILE, n_tok)
    n_tiles = n_tok // tile
    halo = (max(POOL_WINDOWS) // 2) * GRID_W

    for widx, w in enumerate(POOL_WINDOWS):

        @pl.when(gi == widx)
        def _(w=w):
            a = a_ref[...]

            def colsum(k):
                v = u_ref[pl.ds(k * tile, tile), :]
                hi, lo = _split_bf16(v)
                return v, _bdot(a, hi) + _bdot(a, lo)

            def finish(k, v, total, cnt):
                p = (total / cnt.astype(F32) - v).astype(BF16)
                y = _bdot(p, pw_ref[...]) * ps_ref[...]
                o_ref[pl.ds(k * tile, tile), :] = y.astype(o_ref.dtype)

            if grid_rows is None:
                for k in range(n_tiles):
                    v, total = colsum(k)
                    pos = lax.broadcasted_iota(jnp.int32, (tile, POOL_GROUP_DIM), 0) + k * tile
                    finish(k, v, total, _box_count(pos, w, n_tok))
            else:
                zeros = jnp.zeros((halo, POOL_GROUP_DIM), F32)
                pad_ref[pl.ds(0, halo), :] = zeros
                pad_ref[pl.ds(halo + n_tok, halo), :] = zeros
                for k in range(n_tiles):
                    _, total = colsum(k)
                    pad_ref[pl.ds(halo + k * tile, tile), :] = total
                for k in range(n_tiles):
                    acc = None
                    for sh in range(-(w // 2), w // 2):
                        part = pad_ref[pl.ds(halo + k * tile + sh * GRID_W, tile), :]
                        acc = part if acc is None else acc + part
                    tok = lax.broadcasted_iota(jnp.int32, (tile, POOL_GROUP_DIM), 0) + k * tile
                    cnt = (_box_count(tok % GRID_W, w, GRID_W)
                           * _box_count(tok // GRID_W, w, grid_rows))
                    finish(k, u_ref[pl.ds(k * tile, tile), :], acc, cnt)


def _window_matrix(w, n, period):
    t = np.arange(n)[:, None]
    s = np.arange(n)[None, :]
    inside = (s >= t - w // 2) & (s < t - w // 2 + w) & (t // period == s // period)
    return inside.astype(np.float32)


def _pool_branch(u_pool, pool_w_bf, pool_scale, need_ctx, geo):
    batch, seq, ctx_len, t_lat = geo["batch"], geo["seq"], geo["ctx"], geo["t_lat"]
    n_groups = len(POOL_WINDOWS)
    ps = pool_scale.reshape(n_groups, 1, POOL_GROUP_DIM)
    tile = min(POOL_TILE, seq)
    halo = (max(POOL_WINDOWS) // 2) * GRID_W

    def call(name, n_tok, row0, a_mat, grid_rows, scratch):
        a_dim = a_mat.shape[1]
        return pl.pallas_call(
            functools.partial(_pool_kernel, n_tok=n_tok, grid_rows=grid_rows),
            grid=(batch, n_groups),
            in_specs=[
                pl.BlockSpec((n_tok, POOL_GROUP_DIM), lambda b, g: (row0 + b, g)),
                pl.BlockSpec((None, a_dim, a_dim), lambda b, g: (g, 0, 0)),
                pl.BlockSpec((None, POOL_GROUP_DIM, POOL_GROUP_DIM), lambda b, g: (g, 0, 0)),
                pl.BlockSpec((None, 1, POOL_GROUP_DIM), lambda b, g: (g, 0, 0)),
            ],
            out_specs=pl.BlockSpec((n_tok, POOL_GROUP_DIM), lambda b, g: (b, g)),
            out_shape=jax.ShapeDtypeStruct((batch * n_tok, POOL_WIDTH), BF16),
            scratch_shapes=scratch,
            compiler_params=_cparams("parallel", "parallel"),
            name=name,
        )(u_pool, a_mat, pool_w_bf, ps)

    a_lat = jnp.asarray(np.stack([_window_matrix(w, tile, GRID_W) for w in POOL_WINDOWS]), BF16)
    p_lat = call("pool_lat", seq, 0, a_lat, seq // GRID_W,
                 [pltpu.VMEM((seq + 2 * halo, POOL_GROUP_DIM), F32)])
    if not need_ctx:
        return p_lat, None
    a_ctx = jnp.asarray(np.stack([_window_matrix(w, ctx_len, ctx_len) for w in POOL_WINDOWS]), BF16)
    return p_lat, call("pool_ctx", ctx_len, t_lat // ctx_len, a_ctx, None, [])


def _ssm_operators(a_re, a_im, log_dt, b_re, b_im, c_re, c_im, d_skip):
    ng, ns, nj, lc = SSM_GROUPS, SSM_STATE, SSM_GROUP_DIM, SSM_CHUNK
    dt = jnp.exp(log_dt)[..., None]
    mag = jnp.exp(a_re * dt)
    abar_re = mag * jnp.cos(a_im * dt)
    abar_im = mag * jnp.sin(a_im * dt)
    nr, ni = abar_re - 1.0, abar_im
    den = a_re * a_re + a_im * a_im
    f_re = (nr * a_re + ni * a_im) / den
    f_im = (ni * a_re - nr * a_im) / den
    bbar_re = f_re[..., None] * b_re - f_im[..., None] * b_im
    bbar_im = f_re[..., None] * b_im + f_im[..., None] * b_re

    pr, pi = [jnp.ones_like(abar_re)], [jnp.zeros_like(abar_im)]
    for _ in range(lc):
        pr.append(pr[-1] * abar_re - pi[-1] * abar_im)
        pi.append(pr[-2] * abar_im + pi[-1] * abar_re)
    ak_re, ak_im = jnp.stack(pr), jnp.stack(pi)

    ab_re = ak_re[:lc, ..., None] * bbar_re - ak_im[:lc, ..., None] * bbar_im
    ab_im = ak_re[:lc, ..., None] * bbar_im + ak_im[:lc, ..., None] * bbar_re
    kern = (jnp.einsum("dgop,kdgpi->dkgoi", c_re, ab_re, precision=HIGHEST)
            - jnp.einsum("dgop,kdgpi->dkgoi", c_im, ab_im, precision=HIGHEST))

    s_idx = np.arange(lc)[:, None]
    t_idx = np.arange(lc)[None, :]
    lag_f = np.clip(t_idx - s_idx, 0, lc - 1)
    lag_b = np.clip(s_idx - t_idx, 0, lc - 1)
    mask_f = jnp.asarray((t_idx >= s_idx), F32)[:, :, None, None, None]
    mask_b = jnp.asarray((s_idx >= t_idx), F32)[:, :, None, None, None]
    m5 = kern[0][lag_f] * mask_f + kern[1][lag_b] * mask_b
    m5 = jnp.transpose(m5, (2, 0, 4, 1, 3))
    eye = jnp.asarray(np.eye(lc)[:, None, :, None] * np.eye(nj)[None, :, None, :], F32)
    m5 = m5 + eye[None] * d_skip.reshape(ng, 1, nj, 1, 1)
    n_oct, n_g8, n_q = SSM_OCTETS, SSM_GROUPS // SSM_OCTETS, SSM_SLABS // SSM_OCTETS
    width = lc * n_g8 * nj
    eye8 = jnp.asarray(np.eye(n_g8), F32)
    eye_q = jnp.asarray(np.eye(n_q), F32)
    eye2 = jnp.asarray(np.eye(2), F32)
    m6 = m5.reshape(n_oct, n_g8, lc, nj, lc, nj)
    m_mat = jnp.einsum("oesitj,eE->oseitEj", m6, eye8).reshape(n_oct, width, width)

    def st(ab, d, order):
        sel = ab[order, d]
        return jnp.transpose(sel, (1, 0, 3, 2)).reshape(n_oct, n_q, 2, lc, nj, ns)

    fwd_order = np.arange(lc)[::-1]
    bwd_order = np.arange(lc)
    st_parts = jnp.stack([st(ab_re, 0, fwd_order), st(ab_im, 0, fwd_order),
                          st(ab_re, 1, bwd_order), st(ab_im, 1, bwd_order)])
    w_st = jnp.einsum("poqasir,qQ,aA->osqaipQAr", st_parts, eye_q, eye2).reshape(n_oct, width, width)

    def so(d, order):
        ar = ak_re[order, d][:, :, None, :]
        ai = ak_im[order, d][:, :, None, :]
        re = c_re[d][None] * ar - c_im[d][None] * ai
        im = c_re[d][None] * ai + c_im[d][None] * ar
        to_mat = lambda v: jnp.transpose(v, (1, 3, 0, 2)).reshape(n_oct, n_q, 2, ns, lc, nj)
        return to_mat(re), to_mat(-im)

    so_parts = jnp.stack(so(0, np.arange(1, lc + 1)) + so(1, np.arange(lc, 0, -1)))
    w_out = jnp.einsum("poqartj,qQ,aA->opqartQAj", so_parts, eye_q, eye2).reshape(n_oct, width, width)

    def pair_vec(v):
        return v.reshape(SSM_SLABS, 2 * ns)

    a_pow = jnp.stack([pair_vec(ak_re[lc, 0]), pair_vec(ak_im[lc, 0]),
                       pair_vec(ak_re[lc, 1]), pair_vec(ak_im[lc, 1])], axis=1)
    a_pow = jnp.pad(a_pow, ((0, 0), (0, SUBLANES - 4), (0, 0)))
    return m_mat.astype(BF16), w_st.astype(BF16), w_out.astype(BF16), a_pow


def _chunk_rows(u_ref):
    return jnp.concatenate([u_ref[t] for t in range(SSM_CHUNK)], axis=1)


def _ssm_state_kernel(u_ref, w_ref, *s_refs):
    s = _bdot(_chunk_rows(u_ref), w_ref[...])
    part_w = s_refs[0].shape[1]
    for p, ref in enumerate(s_refs):
        ref[...] = s[:, p * part_w:(p + 1) * part_w]


def _ssm_scan_kernel(sfr_ref, sfi_ref, sbr_ref, sbi_ref, a_ref, hfr_ref, hfi_ref, hbr_ref, hbi_ref,
                     *, n_lat_chunks, n_ctx_chunks, batch):
    shape = (batch, LANES)
    afr = jnp.broadcast_to(a_ref[0:1, :], shape)
    afi = jnp.broadcast_to(a_ref[1:2, :], shape)
    abr = jnp.broadcast_to(a_ref[2:3, :], shape)
    abi = jnp.broadcast_to(a_ref[3:4, :], shape)
    zero = jnp.zeros(shape, F32)

    def chain(base, n, carry):
        def step(k, c):
            fr, fi, br, bi = c
            rf = pl.ds(base + k, batch, stride=n)
            rb = pl.ds(base + (n - 1) - k, batch, stride=n)
            hfr_ref[rf, :] = fr
            hfi_ref[rf, :] = fi
            hbr_ref[rb, :] = br
            hbi_ref[rb, :] = bi
            xr, xi = sfr_ref[rf, :], sfi_ref[rf, :]
            yr, yi = sbr_ref[rb, :], sbi_ref[rb, :]
            return (fr * afr - fi * afi + xr, fi * afr + fr * afi + xi,
                    br * abr - bi * abi + yr, bi * abr + br * abi + yi)

        return lax.fori_loop(0, n, step, carry)

    carry = chain(batch * n_lat_chunks, n_ctx_chunks, (zero, zero, zero, zero))
    chain(0, n_lat_chunks, carry)


def _ssm_out_kernel(u_ref, hfr_ref, hfi_ref, hbr_ref, hbi_ref, m_ref, w_ref, y_ref):
    h = jnp.concatenate([hfr_ref[...], hfi_ref[...], hbr_ref[...], hbi_ref[...]], axis=1)
    y = _bdot(_chunk_rows(u_ref), m_ref[...]) + _bdot(h.astype(BF16), w_ref[...])
    for t in range(SSM_CHUNK):
        y_ref[t] = y[:, t * LANES:(t + 1) * LANES].astype(y_ref.dtype)


def _div_tile(n, limit, mult):
    return max(t for t in range(mult, min(n, limit) + 1, mult) if n % t == 0)


def _ssm_branch(u4, ops, geo):
    m_mat, w_st, w_out, a_pow = ops
    batch, seq, ctx_len = geo["batch"], geo["seq"], geo["ctx"]
    lc = SSM_CHUNK
    n_rows = u4.shape[1]
    width = m_mat.shape[1]
    n_parts = 4
    part_w = width // n_parts
    bf16_rows = 2 * SUBLANES
    part_shape = jax.ShapeDtypeStruct((n_rows, SSM_SLABS * LANES), F32)

    def u_blk(rt):
        return pl.BlockSpec((lc, rt, LANES), lambda o, r: (0, r, o))

    def part_blk(rt):
        return pl.BlockSpec((rt, part_w), lambda o, r: (r, o))

    wmat_blk = pl.BlockSpec((None, width, width), lambda o, r: (o, 0, 0))

    rt = _div_tile(n_rows, 544, bf16_rows)
    s_parts = pl.pallas_call(
        _ssm_state_kernel,
        grid=(SSM_OCTETS, n_rows // rt),
        in_specs=[u_blk(rt), wmat_blk],
        out_specs=[part_blk(rt)] * n_parts,
        out_shape=[part_shape] * n_parts,
        compiler_params=_cparams("parallel", "parallel"),
        name="ssm_state",
    )(u4, w_st)

    slab_blk = pl.BlockSpec((n_rows, LANES), lambda q: (0, q))
    h_parts = pl.pallas_call(
        functools.partial(_ssm_scan_kernel, n_lat_chunks=seq // lc, n_ctx_chunks=ctx_len // lc,
                          batch=batch),
        grid=(SSM_SLABS,),
        in_specs=[slab_blk] * n_parts + [pl.BlockSpec((None, SUBLANES, LANES), lambda q: (q, 0, 0))],
        out_specs=[slab_blk] * n_parts,
        out_shape=[part_shape] * n_parts,
        compiler_params=_cparams("parallel"),
        name="ssm_scan",
    )(*s_parts, a_pow)

    rt = _div_tile(n_rows, 272, bf16_rows)
    return pl.pallas_call(
        _ssm_out_kernel,
        grid=(SSM_OCTETS, n_rows // rt),
        in_specs=[u_blk(rt)] + [part_blk(rt)] * n_parts + [wmat_blk, wmat_blk],
        out_specs=u_blk(rt),
        out_shape=jax.ShapeDtypeStruct(u4.shape, BF16),
        compiler_params=_cparams("parallel", "parallel"),
        name="ssm_out",
    )(u4, *h_parts, m_mat, w_out)


def _route(scores_t, biased_t):
    sc = [scores_t[e:e + 1, :] for e in range(N_EXPERTS)]
    bc = [biased_t[e:e + 1, :] for e in range(N_EXPERTS)]
    npg = EXPERTS_PER_GROUP

    group_scores = []
    for g in range(N_EXPERT_GROUPS):
        v = bc[g * npg:(g + 1) * npg]
        best = None
        for i in range(npg):
            for j in range(i + 1, npg):
                pair = v[i] + v[j]
                best = pair if best is None else jnp.maximum(best, pair)
        group_scores.append(best)

    bg = jnp.zeros_like(group_scores[0], dtype=jnp.int32)
    best = group_scores[0]
    for g in range(1, N_EXPERT_GROUPS):
        upd = group_scores[g] > best
        bg = jnp.where(upd, g, bg)
        best = jnp.where(upd, group_scores[g], best)

    vb = list(bc[:npg])
    vs = list(sc[:npg])
    for g in range(1, N_EXPERT_GROUPS):
        sel = bg == g
        for j in range(npg):
            vb[j] = jnp.where(sel, bc[g * npg + j], vb[j])
            vs[j] = jnp.where(sel, sc[g * npg + j], vs[j])

    i1 = jnp.zeros_like(bg)
    m1, w1 = vb[0], vs[0]
    for j in range(1, npg):
        upd = vb[j] > m1
        i1 = jnp.where(upd, j, i1)
        m1 = jnp.where(upd, vb[j], m1)
        w1 = jnp.where(upd, vs[j], w1)

    neg = jnp.full_like(m1, -jnp.inf)
    i2 = jnp.zeros_like(bg)
    m2, w2 = neg, jnp.zeros_like(w1)
    for j in range(npg):
        cand = jnp.where(i1 == j, neg, vb[j])
        upd = cand > m2
        i2 = jnp.where(upd, j, i2)
        m2 = jnp.where(upd, cand, m2)
        w2 = jnp.where(upd, vs[j], w2)

    den = w1 + w2
    w1, w2 = w1 / den, w2 / den
    gates = [jnp.where(i1 == j, w1, 0.0) + jnp.where(i2 == j, w2, 0.0) for j in range(npg)]
    return bg, gates


def _merge_kernel(x_ref, y_ref, gp_ref, gs_ref, mod_ref, g2_ref, gluw_ref, glub_ref,
                  wbp_ref, wbs_ref, wo_ref, rw_ref, rb_ref, plat_ref, *rest, lat_tiles):
    if len(rest) == 5:
        pc_ref, x1_ref, h2_ref, grp_ref, ys_ref = rest
        pooled = jnp.where(pl.program_id(0) < lat_tiles, plat_ref[...], pc_ref[...])
    else:
        x1_ref, h2_ref, grp_ref, ys_ref = rest
        pooled = plat_ref[...]
    n_chunks = ys_ref.shape[1] // SSM_CHUNK
    n_col = ys_ref.shape[0]
    for t in range(SSM_CHUNK):
        yt = y_ref[t].astype(F32)
        for j in range(n_col):
            ys_ref[j, pl.ds(t, n_chunks, stride=SSM_CHUNK), :] = yt[:, j * LANES:(j + 1) * LANES]
    s = _gelu_tanh(jnp.concatenate([ys_ref[j] for j in range(n_col)], axis=1))
    s = s * _sigmoid(_bdot(s.astype(BF16), gluw_ref[...]) + glub_ref[...])
    m = (_sigmoid(gp_ref[...].astype(F32)) * _bdot(pooled, wbp_ref[...])
         + _sigmoid(gs_ref[...].astype(F32)) * _bdot(s.astype(BF16), wbs_ref[...]))
    x1 = x_ref[...] + mod_ref[2:3, :] * _bdot(m.astype(BF16), wo_ref[...])
    x1_ref[...] = x1
    h2 = _rms(x1, g2_ref[...]) * (1.0 + mod_ref[4:5, :]) + mod_ref[3:4, :]
    d = h2.shape[1]
    h2_ref[:, 0:d] = h2

    hi, lo = _split_bf16(h2)
    r1 = _bdot(hi, rw_ref[...])
    logits = r1[:, 0:LANES] + (r1[:, LANES:] + _bdot(lo, rw_ref[:, 0:LANES]))
    scores = _sigmoid(logits)
    bg, gates = _route(scores.T, (scores + rb_ref[...]).T)
    tm = bg.shape[1]
    row = lax.broadcasted_iota(jnp.int32, (SUBLANES, tm), 0)
    grp_ref[...] = jnp.broadcast_to(bg, (SUBLANES, tm))
    g8 = jnp.zeros((SUBLANES, tm), F32)
    for j, g in enumerate(gates):
        g8 = jnp.where(row == j, g, g8)
    g_t = jnp.concatenate([g8, jnp.zeros((LANES - SUBLANES, tm), F32)], axis=0)
    h2_ref[:, d:] = g_t.T


def _merge(x_all, y4, pooled, g_pool, g_ssm, mod_l, g2, wts, router, n_rows, geo):
    d = x_all.shape[1]
    tm = geo["tm"]
    lat_tiles = geo["t_lat"] // tm
    mod_map = lambda i: (_mod_index(i, lat_tiles, geo["seq"] // tm, geo["batch"]), 0, 0)
    row_map = lambda i: (i, 0)
    full = lambda a: pl.BlockSpec(a.shape, lambda i: (0,) * a.ndim)
    glu_w, glu_b, w_bp, w_bs, w_o = wts
    rw, rb = router
    p_lat, p_ctx = pooled
    pooled_specs = [pl.BlockSpec((tm, POOL_WIDTH), lambda i: (jnp.minimum(i, lat_tiles - 1), 0))]
    pooled_args = [p_lat]
    if p_ctx is not None:
        pooled_specs.append(pl.BlockSpec((tm, POOL_WIDTH), lambda i: (jnp.maximum(i - lat_tiles, 0), 0)))
        pooled_args.append(p_ctx)
    return pl.pallas_call(
        functools.partial(_merge_kernel, lat_tiles=lat_tiles),
        grid=(n_rows // tm,),
        in_specs=[
            pl.BlockSpec((tm, d), row_map),
            pl.BlockSpec((SSM_CHUNK, tm // SSM_CHUNK, SSM_WIDTH), lambda i: (0, i, 0)),
            pl.BlockSpec((tm, d), row_map),
            pl.BlockSpec((tm, d), row_map),
            pl.BlockSpec((None, MOD_ROWS, d), mod_map),
            full(g2), full(glu_w), full(glu_b), full(w_bp), full(w_bs), full(w_o), full(rw), full(rb),
        ] + pooled_specs,
        out_specs=[
            pl.BlockSpec((tm, d), row_map),
            pl.BlockSpec((tm, d + LANES), row_map),
            pl.BlockSpec((SUBLANES, tm), lambda i: (0, i)),
        ],
        out_shape=[
            jax.ShapeDtypeStruct((n_rows, d), F32),
            jax.ShapeDtypeStruct((n_rows, d + LANES), F32),
            jax.ShapeDtypeStruct((SUBLANES, n_rows), jnp.int32),
        ],
        scratch_shapes=[pltpu.VMEM((SSM_WIDTH // LANES, tm, LANES), F32)],
        compiler_params=_cparams("parallel"),
        name="merge_router",
    )(x_all, y4, g_pool, g_ssm, mod_l, g2, glu_w, glu_b, w_bp, w_bs, w_o, rw, rb, *pooled_args)


def _moe_plan(group, tile):
    n_rows = group.shape[0]
    n_tiles = n_rows // tile + N_EXPERT_GROUPS
    onehot = (group[:, None] == jnp.arange(N_EXPERT_GROUPS)[None, :]).astype(jnp.int32)
    counts = jnp.sum(onehot, axis=0)
    padded = ((counts + tile - 1) // tile) * tile
    ends = jnp.cumsum(padded)
    starts = ends - padded
    rank = jnp.sum((jnp.cumsum(onehot, axis=0) - 1) * onehot, axis=1)
    pos = jnp.sum(onehot * starts[None, :], axis=1) + rank
    token = jnp.arange(n_rows, dtype=jnp.int32)
    spare = n_rows + jnp.arange(n_tiles * tile, dtype=jnp.int32) % tile
    src = jnp.zeros((n_tiles * tile,), jnp.int32).at[pos].set(token)
    dst = spare.at[pos].set(token)
    tile_start = jnp.arange(n_tiles, dtype=jnp.int32) * tile
    tile_group = jnp.minimum(jnp.sum(tile_start[:, None] >= ends[None, :], axis=1),
                             N_EXPERT_GROUPS - 1).astype(jnp.int32)
    n_used = (ends[-1] // tile).astype(jnp.int32).reshape(1)
    shape3 = (n_tiles, 1, tile)
    return tile_group, n_used, src.reshape(shape3), dst.reshape(shape3)


def _moe_kernel(tg_ref, nu_ref, src_ref, nxt_ref, dst_ref, h_hbm, w1_ref, w3_ref, w2_ref, y_hbm,
                hbuf, ybuf, gsem, ssem, *, tile, d):
    i = pl.program_id(0)
    n_used = nu_ref[0]
    slot = i % 2

    def row_copy_in(idx_ref, r, s):
        return pltpu.make_async_copy(h_hbm.at[pl.ds(idx_ref[0, r], 1)],
                                     hbuf.at[s, pl.ds(r, 1)], gsem.at[s])

    def row_copy_out(r, s):
        return pltpu.make_async_copy(ybuf.at[s, pl.ds(r, 1)],
                                     y_hbm.at[pl.ds(dst_ref[0, r], 1)], ssem.at[s])

    def tile_copy_in(s):
        return pltpu.make_async_copy(h_hbm.at[pl.ds(0, tile)], hbuf.at[s], gsem.at[s])

    def tile_copy_out(s):
        return pltpu.make_async_copy(ybuf.at[s], y_hbm.at[pl.ds(0, tile)], ssem.at[s])

    def start_rows(make):
        def body(r, carry):
            make(r).start()
            return carry
        lax.fori_loop(0, tile, body, 0, unroll=8)

    @pl.when(i == 0)
    def _():
        start_rows(lambda r: row_copy_in(src_ref, r, 0))
        ybuf[1] = jnp.zeros((tile, d), F32)
        spare = pltpu.make_async_copy(ybuf.at[1], y_hbm.at[pl.ds(y_hbm.shape[0] - tile, tile)],
                                      ssem.at[1])
        spare.start()
        spare.wait()

    @pl.when(i + 1 < n_used)
    def _():
        start_rows(lambda r: row_copy_in(nxt_ref, r, 1 - slot))

    @pl.when(i < n_used)
    def _():
        tile_copy_in(slot).wait()
        rows = hbuf[slot]
        h = rows[:, 0:d].astype(BF16)
        gates = rows[:, d:]
        acc = None
        for e in range(EXPERTS_PER_GROUP):
            hid = (_silu(_bdot(h, w1_ref[e])) * _bdot(h, w3_ref[e])).astype(BF16)
            y = gates[:, e:e + 1] * _bdot(hid, w2_ref[e])
            acc = y if acc is None else acc + y

        @pl.when(i >= 2)
        def _():
            tile_copy_out(slot).wait()

        ybuf[slot] = acc
        start_rows(lambda r: row_copy_out(r, slot))

    @pl.when(i == pl.num_programs(0) - 1)
    def _():
        tile_copy_out((n_used - 1) % 2).wait()

        @pl.when(n_used >= 2)
        def _():
            tile_copy_out(n_used % 2).wait()


def _moe(h2g, group, w1, w3, w2, geo):
    n_rows, dg = h2g.shape
    d = dg - LANES
    tile = geo["tm_moe"]
    npg = EXPERTS_PER_GROUP
    d_exp = w1.shape[-1]
    tile_group, n_used, src, dst = _moe_plan(group, tile)
    n_tiles = tile_group.shape[0]
    idx_blk = lambda f: pl.BlockSpec((None, 1, tile), f, memory_space=pltpu.SMEM)
    w_map = lambda i, tg, nu: (tg[i], 0, 0, 0)
    grid_spec = pltpu.PrefetchScalarGridSpec(
        num_scalar_prefetch=2,
        grid=(n_tiles,),
        in_specs=[
            idx_blk(lambda i, tg, nu: (i, 0, 0)),
            idx_blk(lambda i, tg, nu: (jnp.minimum(i + 1, n_tiles - 1), 0, 0)),
            idx_blk(lambda i, tg, nu: (i, 0, 0)),
            pl.BlockSpec(memory_space=pl.ANY),
            pl.BlockSpec((None, npg, d, d_exp), w_map),
            pl.BlockSpec((None, npg, d, d_exp), w_map),
            pl.BlockSpec((None, npg, d_exp, d), w_map),
        ],
        out_specs=pl.BlockSpec(memory_space=pl.ANY),
        scratch_shapes=[
            pltpu.VMEM((2, tile, dg), F32),
            pltpu.VMEM((2, tile, d), F32),
            pltpu.SemaphoreType.DMA((2,)),
            pltpu.SemaphoreType.DMA((2,)),
        ],
    )
    return pl.pallas_call(
        functools.partial(_moe_kernel, tile=tile, d=d),
        grid_spec=grid_spec,
        out_shape=jax.ShapeDtypeStruct((n_rows + tile, d), F32),
        compiler_params=_cparams("arbitrary"),
        name="moe_grouped",
    )(tile_group, n_used, src, src, dst, h2g, w1, w3, w2)


def _residual_kernel(x1_ref, y_ref, mod_ref, fg_ref, o_ref, *, final_norm):
    x2 = x1_ref[...] + mod_ref[5:6, :] * y_ref[...]
    o_ref[...] = _rms(x2, fg_ref[...]) if final_norm else x2


def _moe_residual(x1, y_moe, mod_l, final_g, final_norm, geo):
    n_rows, d = x1.shape
    tm = geo["tm"]
    mod_map = lambda i: (_mod_index(i, geo["t_lat"] // tm, geo["seq"] // tm, geo["batch"]), 0, 0)
    row_map = lambda i: (i, 0)
    return pl.pallas_call(
        functools.partial(_residual_kernel, final_norm=final_norm),
        grid=(n_rows // tm,),
        in_specs=[
            pl.BlockSpec((tm, d), row_map),
            pl.BlockSpec((tm, d), row_map),
            pl.BlockSpec((None, MOD_ROWS, d), mod_map),
            pl.BlockSpec((1, d), lambda i: (0, 0)),
        ],
        out_specs=pl.BlockSpec((tm, d), row_map),
        out_shape=jax.ShapeDtypeStruct((n_rows, d), F32),
        compiler_params=_cparams("parallel"),
        name="moe_residual",
    )(x1, y_moe, mod_l, final_g)


def kernel(x, c, ctx, c_ctx, w_mod, b_mod, norm1_g, norm2_g, w_in, pool_w, pool_scale, ssm_a_re, ssm_a_im, ssm_log_dt, ssm_b_re, ssm_b_im, ssm_c_re, ssm_c_im, ssm_d, glu_w, glu_b, w_branch_pool, w_branch_ssm, w_out, router_w, router_b, expert_w1, expert_w3, expert_w2, final_g):
    batch, seq, d = x.shape
    ctx_len = ctx.shape[1]
    depth = w_mod.shape[0]
    t_lat, t_ctx = batch * seq, batch * ctx_len
    assert seq % GRID_W == 0 and seq % POOL_TILE == 0 and ctx_len % SSM_CHUNK == 0
    assert t_lat % ctx_len == 0 and batch == SUBLANES
    geo = dict(batch=batch, seq=seq, ctx=ctx_len, t_lat=t_lat,
               tm=_pow2_tile(512, seq, t_ctx), tm_moe=_pow2_tile(512, seq, t_ctx))

    n_cond = 2 * SUBLANES
    c_rows = jnp.concatenate([c, c_ctx[None, :], jnp.zeros((n_cond - batch - 1, d), F32)], axis=0)
    mod = _adaln_tables(c_rows, w_mod, b_mod)

    rw = jnp.pad(router_w, ((0, 0), (0, LANES - N_EXPERTS)))
    rw_hi, rw_lo = _split_bf16(rw)
    rw_cat = jnp.concatenate([rw_hi, rw_lo], axis=1)
    rb = jnp.pad(router_b, (0, LANES - N_EXPERTS)).reshape(1, LANES)
    fg = final_g.reshape(1, d)
    d_exp = expert_w1.shape[-1]
    grouped = lambda w, a, b: w.astype(BF16).reshape(N_EXPERT_GROUPS, EXPERTS_PER_GROUP, a, b)

    x_all = jnp.concatenate([x.reshape(t_lat, d), ctx.reshape(t_ctx, d)], axis=0)
    for i in range(depth):
        last = i == depth - 1
        n_rows = t_lat if last else t_lat + t_ctx
        ops = _ssm_operators(ssm_a_re[i], ssm_a_im[i], ssm_log_dt[i], ssm_b_re[i], ssm_b_im[i],
                             ssm_c_re[i], ssm_c_im[i], ssm_d[i])
        u_pool, u4, g_pool, g_ssm = _inproj(
            x_all, mod[i], norm1_g[i].reshape(1, d), w_in[i].astype(BF16), geo)
        pooled = _pool_branch(u_pool, pool_w[i].astype(BF16), pool_scale[i], not last, geo)
        y4 = _ssm_branch(u4, ops, geo)
        wts = (glu_w[i].astype(BF16), glu_b[i].reshape(1, SSM_WIDTH), w_branch_pool[i].astype(BF16),
               w_branch_ssm[i].astype(BF16), w_out[i].astype(BF16))
        x1, h2g, grp = _merge(x_all, y4, pooled, g_pool, g_ssm, mod[i],
                              norm2_g[i].reshape(1, d), wts, (rw_cat, rb), n_rows, geo)
        y_moe = _moe(h2g, grp[0], grouped(expert_w1[i], d, d_exp), grouped(expert_w3[i], d, d_exp),
                     grouped(expert_w2[i], d_exp, d), geo)
        x_all = _moe_residual(x1, y_moe, mod[i], fg, last, geo)
    return x_all.reshape(batch, seq, d)
```

```python
import functools
import math

import numpy as np
import jax
import jax.numpy as jnp
from jax import lax
from jax.experimental import pallas as pl
from jax.experimental.pallas import tpu as pltpu

F32 = jnp.float32
BF16 = jnp.bfloat16
HIGHEST = lax.Precision.HIGHEST

GRID_W = 64
POOL_WINDOWS = (2, 4, 8, 16)
POOL_GROUP_DIM = 128
POOL_WIDTH = 512
SSM_WIDTH = 512
SSM_GROUP_DIM = 16
SSM_GROUPS = 32
SSM_STATE = 64
N_MOD = 6
N_EXPERTS = 16
EXPERTS_PER_GROUP = 4
N_EXPERT_GROUPS = 4
RMS_EPS = 1e-6

SSM_CHUNK = 16
SSM_OCTETS = SSM_GROUPS // 8
SSM_SLABS = SSM_GROUPS // 2
POOL_TILE = 256
LANES = 128
SUBLANES = 8
V7X_VMEM_LIMIT = 56 * 1024 * 1024
MOD_ROWS = 8


def _cparams(*sem):
    return pltpu.CompilerParams(dimension_semantics=sem, vmem_limit_bytes=V7X_VMEM_LIMIT)


def _pow2_tile(limit, *sizes):
    t = limit
    while any(s % t for s in sizes):
        t //= 2
    return t


def _sigmoid(v):
    return 0.5 * jnp.tanh(0.5 * v) + 0.5


def _silu(v):
    return v * _sigmoid(v)


def _gelu_tanh(v):
    c = math.sqrt(2.0 / math.pi)
    return 0.5 * v * (1.0 + jnp.tanh(c * (v + 0.044715 * (v * v * v))))


def _rms(x, g):
    ms = jnp.mean(x * x, axis=-1, keepdims=True)
    return x * lax.rsqrt(ms + RMS_EPS) * g


def _bdot(a, b):
    return jnp.dot(a, b, preferred_element_type=F32)


def _mod_kernel(c_ref, w_ref, b_ref, o_ref):
    s = _silu(c_ref[...])
    o_ref[...] = jnp.dot(s, w_ref[...], preferred_element_type=F32, precision=HIGHEST) + b_ref[...]


def _adaln_tables(c_rows, w_mod, b_mod):
    depth, d, _ = w_mod.shape
    nrow = c_rows.shape[0]
    out = pl.pallas_call(
        _mod_kernel,
        grid=(depth, N_MOD),
        in_specs=[
            pl.BlockSpec((nrow, d), lambda l, j: (0, 0)),
            pl.BlockSpec((None, d, d), lambda l, j: (l, 0, j)),
            pl.BlockSpec((None, None, 1, d), lambda l, j: (l, j, 0, 0)),
        ],
        out_specs=pl.BlockSpec((None, None, nrow, d), lambda l, j: (l, j, 0, 0)),
        out_shape=jax.ShapeDtypeStruct((depth, N_MOD, nrow, d), F32),
        compiler_params=_cparams("parallel", "parallel"),
        name="adaln_tables",
    )(c_rows, w_mod, b_mod.reshape(depth, N_MOD, 1, d))
    out = jnp.transpose(out, (0, 2, 1, 3))
    return jnp.pad(out, ((0, 0), (0, 0), (0, MOD_ROWS - N_MOD), (0, 0)))


def _inproj_kernel(x_ref, mod_ref, g_ref, w_ref, up_ref, us_ref, gp_ref, gs_ref, zs_ref):
    h = _rms(x_ref[...], g_ref[...]) * (1.0 + mod_ref[1:2, :]) + mod_ref[0:1, :]
    hb = h.astype(BF16)
    d = gp_ref.shape[1]
    o0, o1, o2 = POOL_WIDTH, POOL_WIDTH + SSM_WIDTH, POOL_WIDTH + SSM_WIDTH + d
    up_ref[...] = _bdot(hb, w_ref[:, 0:o0])
    gp_ref[...] = _bdot(hb, w_ref[:, o1:o2]).astype(BF16)
    gs_ref[...] = _bdot(hb, w_ref[:, o2:]).astype(BF16)
    zs = _bdot(hb, w_ref[:, o0:o1])
    n_chunks = zs_ref.shape[1] // SSM_CHUNK
    for j in range(zs_ref.shape[0]):
        zs_ref[j] = zs[:, j * LANES:(j + 1) * LANES]
    for t in range(SSM_CHUNK):
        for j in range(zs_ref.shape[0]):
            piece = zs_ref[j, pl.ds(t, n_chunks, stride=SSM_CHUNK), :]
            us_ref[t, :, j * LANES:(j + 1) * LANES] = piece.astype(BF16)


def _mod_index(i, lat_tiles, tiles_per_batch, n_batch):
    return jnp.where(i < lat_tiles, i // tiles_per_batch, n_batch)


def _inproj(x_all, mod_l, g1, w_in_bf, geo):
    rows, d = x_all.shape
    tm = geo["tm"]
    mod_map = lambda i: (_mod_index(i, geo["t_lat"] // tm, geo["seq"] // tm, geo["batch"]), 0, 0)
    row_map = lambda i: (i, 0)
    return pl.pallas_call(
        _inproj_kernel,
        grid=(rows // tm,),
        in_specs=[
            pl.BlockSpec((tm, d), row_map),
            pl.BlockSpec((None, MOD_ROWS, d), mod_map),
            pl.BlockSpec((1, d), lambda i: (0, 0)),
            pl.BlockSpec(w_in_bf.shape, lambda i: (0, 0)),
        ],
        out_specs=[
            pl.BlockSpec((tm, POOL_WIDTH), row_map),
            pl.BlockSpec((SSM_CHUNK, tm // SSM_CHUNK, SSM_WIDTH), lambda i: (0, i, 0)),
            pl.BlockSpec((tm, d), row_map),
            pl.BlockSpec((tm, d), row_map),
        ],
        out_shape=[
            jax.ShapeDtypeStruct((rows, POOL_WIDTH), F32),
            jax.ShapeDtypeStruct((SSM_CHUNK, rows // SSM_CHUNK, SSM_WIDTH), BF16),
            jax.ShapeDtypeStruct((rows, d), BF16),
            jax.ShapeDtypeStruct((rows, d), BF16),
        ],
        scratch_shapes=[pltpu.VMEM((SSM_WIDTH // LANES, tm, LANES), F32)],
        compiler_params=_cparams("parallel"),
        name="inproj",
    )(x_all, mod_l, g1, w_in_bf)


def _box_count(pos, w, n):
    return jnp.minimum(pos + w // 2, n) - jnp.maximum(pos - w // 2, 0)


def _split_bf16(v):
    hi = v.astype(BF16)
    lo = (v - hi.astype(F32)).astype(BF16)
    return hi, lo


def _pool_kernel(u_ref, a_ref, pw_ref, ps_ref, *rest, n_tok, grid_rows):
    o_ref = rest[0]
    pad_ref = rest[1] if grid_rows is not None else None
    gi = pl.program_id(1)
    tile = min(POOL_TILE, n_tok)
    n_tiles = n_tok // tile
    halo = (max(POOL_WINDOWS) // 2) * GRID_W

    for widx, w in enumerate(POOL_WINDOWS):

        @pl.when(gi == widx)
        def _(w=w):
            a = a_ref[...]

            def colsum(k):
                v = u_ref[pl.ds(k * tile, tile), :]
                hi, lo = _split_bf16(v)
                return v, _bdot(a, hi) + _bdot(a, lo)

            def finish(k, v, total, cnt):
                p = (total / cnt.astype(F32) - v).astype(BF16)
                y = _bdot(p, pw_ref[...]) * ps_ref[...]
                o_ref[pl.ds(k * tile, tile), :] = y.astype(o_ref.dtype)

            if grid_rows is None:
                for k in range(n_tiles):
                    v, total = colsum(k)
                    pos = lax.broadcasted_iota(jnp.int32, (tile, POOL_GROUP_DIM), 0) + k * tile
                    finish(k, v, total, _box_count(pos, w, n_tok))
            else:
                zeros = jnp.zeros((halo, POOL_GROUP_DIM), F32)
                pad_ref[pl.ds(0, halo), :] = zeros
                pad_ref[pl.ds(halo + n_tok, halo), :] = zeros
                for k in range(n_tiles):
                    _, total = colsum(k)
                    pad_ref[pl.ds(halo + k * tile, tile), :] = total
                for k in range(n_tiles):
                    acc = None
                    for sh in range(-(w // 2), w // 2):
                        part = pad_ref[pl.ds(halo + k * tile + sh * GRID_W, tile), :]
                        acc = part if acc is None else acc + part
                    tok = lax.broadcasted_iota(jnp.int32, (tile, POOL_GROUP_DIM), 0) + k * tile
                    cnt = (_box_count(tok % GRID_W, w, GRID_W)
                           * _box_count(tok // GRID_W, w, grid_rows))
                    finish(k, u_ref[pl.ds(k * tile, tile), :], acc, cnt)


def _window_matrix(w, n, period):
    t = np.arange(n)[:, None]
    s = np.arange(n)[None, :]
    inside = (s >= t - w // 2) & (s < t - w // 2 + w) & (t // period == s // period)
    return inside.astype(np.float32)


def _pool_branch(u_pool, pool_w_bf, pool_scale, need_ctx, geo):
    batch, seq, ctx_len, t_lat = geo["batch"], geo["seq"], geo["ctx"], geo["t_lat"]
    n_groups = len(POOL_WINDOWS)
    ps = pool_scale.reshape(n_groups, 1, POOL_GROUP_DIM)
    tile = min(POOL_TILE, seq)
    halo = (max(POOL_WINDOWS) // 2) * GRID_W

    def call(name, n_tok, row0, a_mat, grid_rows, scratch):
        a_dim = a_mat.shape[1]
        return pl.pallas_call(
            functools.partial(_pool_kernel, n_tok=n_tok, grid_rows=grid_rows),
            grid=(batch, n_groups),
            in_specs=[
                pl.BlockSpec((n_tok, POOL_GROUP_DIM), lambda b, g: (row0 + b, g)),
                pl.BlockSpec((None, a_dim, a_dim), lambda b, g: (g, 0, 0)),
                pl.BlockSpec((None, POOL_GROUP_DIM, POOL_GROUP_DIM), lambda b, g: (g, 0, 0)),
                pl.BlockSpec((None, 1, POOL_GROUP_DIM), lambda b, g: (g, 0, 0)),
            ],
            out_specs=pl.BlockSpec((n_tok, POOL_GROUP_DIM), lambda b, g: (b, g)),
            out_shape=jax.ShapeDtypeStruct((batch * n_tok, POOL_WIDTH), BF16),
            scratch_shapes=scratch,
            compiler_params=_cparams("parallel", "parallel"),
            name=name,
        )(u_pool, a_mat, pool_w_bf, ps)

    a_lat = jnp.asarray(np.stack([_window_matrix(w, tile, GRID_W) for w in POOL_WINDOWS]), BF16)
    p_lat = call("pool_lat", seq, 0, a_lat, seq // GRID_W,
                 [pltpu.VMEM((seq + 2 * halo, POOL_GROUP_DIM), F32)])
    if not need_ctx:
        return p_lat, None
    a_ctx = jnp.asarray(np.stack([_window_matrix(w, ctx_len, ctx_len) for w in POOL_WINDOWS]), BF16)
    return p_lat, call("pool_ctx", ctx_len, t_lat // ctx_len, a_ctx, None, [])


def _ssm_operators(a_re, a_im, log_dt, b_re, b_im, c_re, c_im, d_skip):
    ng, ns, nj, lc = SSM_GROUPS, SSM_STATE, SSM_GROUP_DIM, SSM_CHUNK
    dt = jnp.exp(log_dt)[..., None]
    mag = jnp.exp(a_re * dt)
    abar_re = mag * jnp.cos(a_im * dt)
    abar_im = mag * jnp.sin(a_im * dt)
    nr, ni = abar_re - 1.0, abar_im
    den = a_re * a_re + a_im * a_im
    f_re = (nr * a_re + ni * a_im) / den
    f_im = (ni * a_re - nr * a_im) / den
    bbar_re = f_re[..., None] * b_re - f_im[..., None] * b_im
    bbar_im = f_re[..., None] * b_im + f_im[..., None] * b_re

    pr, pi = [jnp.ones_like(abar_re)], [jnp.zeros_like(abar_im)]
    for _ in range(lc):
        pr.append(pr[-1] * abar_re - pi[-1] * abar_im)
        pi.append(pr[-2] * abar_im + pi[-1] * abar_re)
    ak_re, ak_im = jnp.stack(pr), jnp.stack(pi)

    ab_re = ak_re[:lc, ..., None] * bbar_re - ak_im[:lc, ..., None] * bbar_im
    ab_im = ak_re[:lc, ..., None] * bbar_im + ak_im[:lc, ..., None] * bbar_re
    kern = (jnp.einsum("dgop,kdgpi->dkgoi", c_re, ab_re, precision=HIGHEST)
            - jnp.einsum("dgop,kdgpi->dkgoi", c_im, ab_im, precision=HIGHEST))

    s_idx = np.arange(lc)[:, None]
    t_idx = np.arange(lc)[None, :]
    lag_f = np.clip(t_idx - s_idx, 0, lc - 1)
    lag_b = np.clip(s_idx - t_idx, 0, lc - 1)
    mask_f = jnp.asarray((t_idx >= s_idx), F32)[:, :, None, None, None]
    mask_b = jnp.asarray((s_idx >= t_idx), F32)[:, :, None, None, None]
    n_oct, n_g8 = SSM_OCTETS, SSM_GROUPS // SSM_OCTETS
    width = lc * n_g8 * nj
    m5 = kern[0][lag_f] * mask_f + kern[1][lag_b] * mask_b
    m6 = jnp.transpose(m5.reshape(lc, lc, n_oct, n_g8, nj, nj), (2, 0, 3, 5, 1, 4))
    skip = jnp.asarray(np.eye(lc)[None, :, None, None, :, None]
                       * np.eye(nj)[None, None, None, :, None, :], F32)
    m6 = m6 + skip * d_skip.reshape(n_oct, 1, n_g8, nj, 1, 1)

    def expand(compact, row_axis):
        eye_shape = [1] * (compact.ndim + 1)
        eye_shape[row_axis], eye_shape[-2] = n_g8, n_g8
        full = compact[..., None, :] * jnp.asarray(np.eye(n_g8).reshape(eye_shape), F32)
        return full.reshape(n_oct, width, width).astype(BF16)

    m_mat = expand(m6, 2)

    def st(ab, d, order):
        sel = ab[order, d].reshape(lc, n_oct, n_g8, ns, nj)
        return jnp.transpose(sel, (1, 0, 2, 4, 3))

    fwd_order = np.arange(lc)[::-1]
    bwd_order = np.arange(lc)
    st_c = jnp.stack([st(ab_re, 0, fwd_order), st(ab_im, 0, fwd_order),
                      st(ab_re, 1, bwd_order), st(ab_im, 1, bwd_order)], axis=4)
    w_st = expand(st_c, 2)

    def so(d, order):
        ar = ak_re[order, d][:, :, None, :]
        ai = ak_im[order, d][:, :, None, :]
        re = c_re[d][None] * ar - c_im[d][None] * ai
        im = c_re[d][None] * ai + c_im[d][None] * ar
        to_c = lambda v: jnp.transpose(v.reshape(lc, n_oct, n_g8, nj, ns), (1, 2, 4, 0, 3))
        return to_c(re), to_c(-im)

    so_c = jnp.stack(so(0, np.arange(1, lc + 1)) + so(1, np.arange(lc, 0, -1)), axis=1)
    w_out = expand(so_c, 2)

    def pair_vec(v):
        return v.reshape(SSM_SLABS, 2 * ns)

    a_pow = jnp.stack([pair_vec(ak_re[lc, 0]), pair_vec(ak_im[lc, 0]),
                       pair_vec(ak_re[lc, 1]), pair_vec(ak_im[lc, 1])], axis=1)
    a_pow = jnp.pad(a_pow, ((0, 0), (0, SUBLANES - 4), (0, 0)))
    return m_mat, w_st, w_out, a_pow


def _chunk_rows(u_ref):
    return jnp.concatenate([u_ref[t] for t in range(SSM_CHUNK)], axis=1)


def _ssm_state_kernel(u_ref, w_ref, *s_refs):
    s = _bdot(_chunk_rows(u_ref), w_ref[...])
    part_w = s_refs[0].shape[1]
    for p, ref in enumerate(s_refs):
        ref[...] = s[:, p * part_w:(p + 1) * part_w]


def _ssm_scan_kernel(sfr_ref, sfi_ref, sbr_ref, sbi_ref, a_ref, hfr_ref, hfi_ref, hbr_ref, hbi_ref,
                     *, n_lat_chunks, n_ctx_chunks, batch):
    shape = (batch, LANES)
    afr = jnp.broadcast_to(a_ref[0:1, :], shape)
    afi = jnp.broadcast_to(a_ref[1:2, :], shape)
    abr = jnp.broadcast_to(a_ref[2:3, :], shape)
    abi = jnp.broadcast_to(a_ref[3:4, :], shape)
    zero = jnp.zeros(shape, F32)

    def chain(base, n, carry):
        def step(k, c):
            fr, fi, br, bi = c
            rf = pl.ds(base + k, batch, stride=n)
            rb = pl.ds(base + (n - 1) - k, batch, stride=n)
            hfr_ref[rf, :] = fr
            hfi_ref[rf, :] = fi
            hbr_ref[rb, :] = br
            hbi_ref[rb, :] = bi
            xr, xi = sfr_ref[rf, :], sfi_ref[rf, :]
            yr, yi = sbr_ref[rb, :], sbi_ref[rb, :]
            return (fr * afr - fi * afi + xr, fi * afr + fr * afi + xi,
                    br * abr - bi * abi + yr, bi * abr + br * abi + yi)

        return lax.fori_loop(0, n, step, carry, unroll=2)

    carry = chain(batch * n_lat_chunks, n_ctx_chunks, (zero, zero, zero, zero))
    chain(0, n_lat_chunks, carry)


def _ssm_out_kernel(u_ref, hfr_ref, hfi_ref, hbr_ref, hbi_ref, m_ref, w_ref, y_ref):
    h = jnp.concatenate([hfr_ref[...], hfi_ref[...], hbr_ref[...], hbi_ref[...]], axis=1)
    y = _bdot(_chunk_rows(u_ref), m_ref[...]) + _bdot(h.astype(BF16), w_ref[...])
    for t in range(SSM_CHUNK):
        y_ref[t] = y[:, t * LANES:(t + 1) * LANES].astype(y_ref.dtype)


def _div_tile(n, limit, mult):
    return max(t for t in range(mult, min(n, limit) + 1, mult) if n % t == 0)


def _ssm_branch(u4, ops, geo):
    m_mat, w_st, w_out, a_pow = ops
    batch, seq, ctx_len = geo["batch"], geo["seq"], geo["ctx"]
    lc = SSM_CHUNK
    n_rows = u4.shape[1]
    width = m_mat.shape[1]
    n_parts = 4
    part_w = width // n_parts
    bf16_rows = 2 * SUBLANES
    part_shape = jax.ShapeDtypeStruct((n_rows, SSM_SLABS * LANES), F32)

    def u_blk(rt):
        return pl.BlockSpec((lc, rt, LANES), lambda o, r: (0, r, o))

    def part_blk(rt):
        return pl.BlockSpec((rt, part_w), lambda o, r: (r, o))

    wmat_blk = pl.BlockSpec((None, width, width), lambda o, r: (o, 0, 0))

    rt = _div_tile(n_rows, 544, bf16_rows)
    s_parts = pl.pallas_call(
        _ssm_state_kernel,
        grid=(SSM_OCTETS, n_rows // rt),
        in_specs=[u_blk(rt), wmat_blk],
        out_specs=[part_blk(rt)] * n_parts,
        out_shape=[part_shape] * n_parts,
        compiler_params=_cparams("parallel", "parallel"),
        name="ssm_state",
    )(u4, w_st)

    slab_blk = pl.BlockSpec((n_rows, LANES), lambda q: (0, q))
    h_parts = pl.pallas_call(
        functools.partial(_ssm_scan_kernel, n_lat_chunks=seq // lc, n_ctx_chunks=ctx_len // lc,
                          batch=batch),
        grid=(SSM_SLABS,),
        in_specs=[slab_blk] * n_parts + [pl.BlockSpec((None, SUBLANES, LANES), lambda q: (q, 0, 0))],
        out_specs=[slab_blk] * n_parts,
        out_shape=[part_shape] * n_parts,
        compiler_params=_cparams("parallel"),
        name="ssm_scan",
    )(*s_parts, a_pow)

    rt = _div_tile(n_rows, 272, bf16_rows)
    return pl.pallas_call(
        _ssm_out_kernel,
        grid=(SSM_OCTETS, n_rows // rt),
        in_specs=[u_blk(rt)] + [part_blk(rt)] * n_parts + [wmat_blk, wmat_blk],
        out_specs=u_blk(rt),
        out_shape=jax.ShapeDtypeStruct(u4.shape, BF16),
        compiler_params=_cparams("parallel", "parallel"),
        name="ssm_out",
    )(u4, *h_parts, m_mat, w_out)


def _route(scores_t, biased_t):
    sc = [scores_t[e:e + 1, :] for e in range(N_EXPERTS)]
    bc = [biased_t[e:e + 1, :] for e in range(N_EXPERTS)]
    npg = EXPERTS_PER_GROUP

    group_scores = []
    for g in range(N_EXPERT_GROUPS):
        v = bc[g * npg:(g + 1) * npg]
        best = None
        for i in range(npg):
            for j in range(i + 1, npg):
                pair = v[i] + v[j]
                best = pair if best is None else jnp.maximum(best, pair)
        group_scores.append(best)

    bg = jnp.zeros_like(group_scores[0], dtype=jnp.int32)
    best = group_scores[0]
    for g in range(1, N_EXPERT_GROUPS):
        upd = group_scores[g] > best
        bg = jnp.where(upd, g, bg)
        best = jnp.where(upd, group_scores[g], best)

    vb = list(bc[:npg])
    vs = list(sc[:npg])
    for g in range(1, N_EXPERT_GROUPS):
        sel = bg == g
        for j in range(npg):
            vb[j] = jnp.where(sel, bc[g * npg + j], vb[j])
            vs[j] = jnp.where(sel, sc[g * npg + j], vs[j])

    i1 = jnp.zeros_like(bg)
    m1, w1 = vb[0], vs[0]
    for j in range(1, npg):
        upd = vb[j] > m1
        i1 = jnp.where(upd, j, i1)
        m1 = jnp.where(upd, vb[j], m1)
        w1 = jnp.where(upd, vs[j], w1)

    neg = jnp.full_like(m1, -jnp.inf)
    i2 = jnp.zeros_like(bg)
    m2, w2 = neg, jnp.zeros_like(w1)
    for j in range(npg):
        cand = jnp.where(i1 == j, neg, vb[j])
        upd = cand > m2
        i2 = jnp.where(upd, j, i2)
        m2 = jnp.where(upd, cand, m2)
        w2 = jnp.where(upd, vs[j], w2)

    den = w1 + w2
    w1, w2 = w1 / den, w2 / den
    gates = [jnp.where(i1 == j, w1, 0.0) + jnp.where(i2 == j, w2, 0.0) for j in range(npg)]
    return bg, gates


def _merge_kernel(x_ref, y_ref, gp_ref, gs_ref, mod_ref, g2_ref, gluw_ref, glub_ref,
                  wbp_ref, wbs_ref, wo_ref, rw_ref, rb_ref, plat_ref, *rest, lat_tiles):
    if len(rest) == 5:
        pc_ref, x1_ref, h2_ref, grp_ref, ys_ref = rest
        pooled = jnp.where(pl.program_id(0) < lat_tiles, plat_ref[...], pc_ref[...])
    else:
        x1_ref, h2_ref, grp_ref, ys_ref = rest
        pooled = plat_ref[...]
    n_chunks = ys_ref.shape[1] // SSM_CHUNK
    n_col = ys_ref.shape[0]
    for t in range(SSM_CHUNK):
        yt = y_ref[t].astype(F32)
        for j in range(n_col):
            ys_ref[j, pl.ds(t, n_chunks, stride=SSM_CHUNK), :] = yt[:, j * LANES:(j + 1) * LANES]
    s = _gelu_tanh(jnp.concatenate([ys_ref[j] for j in range(n_col)], axis=1))
    s = s * _sigmoid(_bdot(s.astype(BF16), gluw_ref[...]) + glub_ref[...])
    m = (_sigmoid(gp_ref[...].astype(F32)) * _bdot(pooled, wbp_ref[...])
         + _sigmoid(gs_ref[...].astype(F32)) * _bdot(s.astype(BF16), wbs_ref[...]))
    x1 = x_ref[...] + mod_ref[2:3, :] * _bdot(m.astype(BF16), wo_ref[...])
    x1_ref[...] = x1
    h2 = _rms(x1, g2_ref[...]) * (1.0 + mod_ref[4:5, :]) + mod_ref[3:4, :]
    tm, d = h2.shape
    n_chunk = d // LANES + 1
    for c in range(n_chunk - 1):
        h2_ref[pl.ds(c, tm, stride=n_chunk), :] = h2[:, c * LANES:(c + 1) * LANES]

    hi, lo = _split_bf16(h2)
    r1 = _bdot(hi, rw_ref[...])
    logits = r1[:, 0:LANES] + (r1[:, LANES:] + _bdot(lo, rw_ref[:, 0:LANES]))
    scores = _sigmoid(logits)
    bg, gates = _route(scores.T, (scores + rb_ref[...]).T)
    row = lax.broadcasted_iota(jnp.int32, (SUBLANES, tm), 0)
    grp_ref[...] = jnp.broadcast_to(bg, (SUBLANES, tm))
    g8 = jnp.zeros((SUBLANES, tm), F32)
    for j, g in enumerate(gates):
        g8 = jnp.where(row == j, g, g8)
    g_t = jnp.concatenate([g8, jnp.zeros((LANES - SUBLANES, tm), F32)], axis=0)
    h2_ref[pl.ds(n_chunk - 1, tm, stride=n_chunk), :] = g_t.T


def _merge(x_all, y4, pooled, g_pool, g_ssm, mod_l, g2, wts, router, n_rows, geo):
    d = x_all.shape[1]
    tm = geo["tm"]
    n_chunk = d // LANES + 1
    lat_tiles = geo["t_lat"] // tm
    mod_map = lambda i: (_mod_index(i, lat_tiles, geo["seq"] // tm, geo["batch"]), 0, 0)
    row_map = lambda i: (i, 0)
    full = lambda a: pl.BlockSpec(a.shape, lambda i: (0,) * a.ndim)
    glu_w, glu_b, w_bp, w_bs, w_o = wts
    rw, rb = router
    p_lat, p_ctx = pooled
    pooled_specs = [pl.BlockSpec((tm, POOL_WIDTH), lambda i: (jnp.minimum(i, lat_tiles - 1), 0))]
    pooled_args = [p_lat]
    if p_ctx is not None:
        pooled_specs.append(pl.BlockSpec((tm, POOL_WIDTH), lambda i: (jnp.maximum(i - lat_tiles, 0), 0)))
        pooled_args.append(p_ctx)
    return pl.pallas_call(
        functools.partial(_merge_kernel, lat_tiles=lat_tiles),
        grid=(n_rows // tm,),
        in_specs=[
            pl.BlockSpec((tm, d), row_map),
            pl.BlockSpec((SSM_CHUNK, tm // SSM_CHUNK, SSM_WIDTH), lambda i: (0, i, 0)),
            pl.BlockSpec((tm, d), row_map),
            pl.BlockSpec((tm, d), row_map),
            pl.BlockSpec((None, MOD_ROWS, d), mod_map),
            full(g2), full(glu_w), full(glu_b), full(w_bp), full(w_bs), full(w_o), full(rw), full(rb),
        ] + pooled_specs,
        out_specs=[
            pl.BlockSpec((tm, d), row_map),
            pl.BlockSpec((tm * n_chunk, LANES), row_map),
            pl.BlockSpec((SUBLANES, tm), lambda i: (0, i)),
        ],
        out_shape=[
            jax.ShapeDtypeStruct((n_rows, d), F32),
            jax.ShapeDtypeStruct((n_rows * n_chunk, LANES), F32),
            jax.ShapeDtypeStruct((SUBLANES, n_rows), jnp.int32),
        ],
        scratch_shapes=[pltpu.VMEM((SSM_WIDTH // LANES, tm, LANES), F32)],
        compiler_params=_cparams("parallel"),
        name="merge_router",
    )(x_all, y4, g_pool, g_ssm, mod_l, g2, glu_w, glu_b, w_bp, w_bs, w_o, rw, rb, *pooled_args)


def _moe_plan(group, tile):
    n_rows = group.shape[0]
    n_tiles = n_rows // tile + N_EXPERT_GROUPS
    onehot = (group[:, None] == jnp.arange(N_EXPERT_GROUPS)[None, :]).astype(jnp.int32)
    counts = jnp.sum(onehot, axis=0)
    padded = ((counts + tile - 1) // tile) * tile
    ends = jnp.cumsum(padded)
    starts = ends - padded
    rank = jnp.sum((jnp.cumsum(onehot, axis=0) - 1) * onehot, axis=1)
    pos = jnp.sum(onehot * starts[None, :], axis=1) + rank
    token = jnp.arange(n_rows, dtype=jnp.int32)
    slot = jnp.arange(n_tiles * tile, dtype=jnp.int32)
    src = jnp.zeros((n_tiles * tile,), jnp.int32).at[pos].set(token)
    slot_group = jnp.minimum(jnp.sum(slot[:, None] >= ends[None, :], axis=1), N_EXPERT_GROUPS - 1)
    used = slot < (starts + counts)[slot_group]
    dst = jnp.where(used, src, n_rows + slot % tile)
    tile_start = jnp.arange(n_tiles, dtype=jnp.int32) * tile
    tile_group = jnp.minimum(jnp.sum(tile_start[:, None] >= ends[None, :], axis=1),
                             N_EXPERT_GROUPS - 1).astype(jnp.int32)
    n_used = (ends[-1] // tile).astype(jnp.int32).reshape(1)
    shape3 = (n_tiles, 1, tile)
    return tile_group, n_used, src.reshape(shape3), dst.reshape(shape3)


def _moe_kernel(tg_ref, nu_ref, src_ref, nxt_ref, dst_ref, h_hbm, w1_ref, w3_ref, w2_ref, y_hbm,
                hbuf, ybuf, gsem, ssem, *, tile, n_in, n_out):
    i = pl.program_id(0)
    n_used = nu_ref[0]
    slot = i % 2

    def row_copy_in(idx_ref, r, s):
        return pltpu.make_async_copy(h_hbm.at[pl.ds(idx_ref[0, r], n_in)],
                                     hbuf.at[s, pl.ds(r * n_in, n_in)], gsem.at[s])

    def row_copy_out(r, s):
        first = pl.multiple_of(dst_ref[0, r], n_out)
        return pltpu.make_async_copy(ybuf.at[s, pl.ds(pl.multiple_of(r * n_out, n_out), n_out)],
                                     y_hbm.at[pl.ds(first, n_out)], ssem.at[s])

    def tile_copy_in(s):
        return pltpu.make_async_copy(h_hbm.at[pl.ds(0, tile * n_in)], hbuf.at[s], gsem.at[s])

    def tile_copy_out(s):
        return pltpu.make_async_copy(ybuf.at[s], y_hbm.at[pl.ds(0, tile * n_out)], ssem.at[s])

    def start_rows(make):
        def body(r, carry):
            make(r).start()
            return carry
        lax.fori_loop(0, tile, body, 0, unroll=8)

    @pl.when(i == 0)
    def _():
        start_rows(lambda r: row_copy_in(src_ref, r, 0))
        ybuf[1] = jnp.zeros(ybuf.shape[1:], F32)
        spare = pltpu.make_async_copy(
            ybuf.at[1], y_hbm.at[pl.ds(y_hbm.shape[0] - tile * n_out, tile * n_out)], ssem.at[1])
        spare.start()
        spare.wait()

    @pl.when(i + 1 < n_used)
    def _():
        start_rows(lambda r: row_copy_in(nxt_ref, r, 1 - slot))

    @pl.when(i < n_used)
    def _():
        tile_copy_in(slot).wait()
        hrows, yrows = hbuf.at[slot], ybuf.at[slot]
        h = jnp.concatenate([hrows[pl.ds(c, tile, stride=n_in), :] for c in range(n_out)],
                            axis=1).astype(BF16)
        gates = hrows[pl.ds(n_in - 1, tile, stride=n_in), :]
        acc = None
        for e in range(EXPERTS_PER_GROUP):
            hid = (_silu(_bdot(h, w1_ref[e])) * _bdot(h, w3_ref[e])).astype(BF16)
            y = gates[:, e:e + 1] * _bdot(hid, w2_ref[e])
            acc = y if acc is None else acc + y

        @pl.when(i >= 2)
        def _():
            tile_copy_out(slot).wait()

        for c in range(n_out):
            yrows[pl.ds(c, tile, stride=n_out), :] = acc[:, c * LANES:(c + 1) * LANES]
        start_rows(lambda r: row_copy_out(r, slot))

    @pl.when(i == pl.num_programs(0) - 1)
    def _():
        tile_copy_out((n_used - 1) % 2).wait()

        @pl.when(n_used >= 2)
        def _():
            tile_copy_out(n_used % 2).wait()


def _moe(h2g, group, w1, w3, w2, geo):
    npg = EXPERTS_PER_GROUP
    d, d_exp = w1.shape[-2:]
    n_out = d // LANES
    n_in = n_out + 1
    n_rows = h2g.shape[0] // n_in
    tile = geo["tm_moe"]
    tile_group, n_used, src, dst = _moe_plan(group, tile)
    src, dst = src * n_in, dst * n_out
    n_tiles = tile_group.shape[0]
    idx_blk = lambda f: pl.BlockSpec((None, 1, tile), f, memory_space=pltpu.SMEM)
    w_map = lambda i, tg, nu: (tg[i], 0, 0, 0)
    grid_spec = pltpu.PrefetchScalarGridSpec(
        num_scalar_prefetch=2,
        grid=(n_tiles,),
        in_specs=[
            idx_blk(lambda i, tg, nu: (i, 0, 0)),
            idx_blk(lambda i, tg, nu: (jnp.minimum(i + 1, n_tiles - 1), 0, 0)),
            idx_blk(lambda i, tg, nu: (i, 0, 0)),
            pl.BlockSpec(memory_space=pl.ANY),
            pl.BlockSpec((None, npg, d, d_exp), w_map),
            pl.BlockSpec((None, npg, d, d_exp), w_map),
            pl.BlockSpec((None, npg, d_exp, d), w_map),
        ],
        out_specs=pl.BlockSpec(memory_space=pl.ANY),
        scratch_shapes=[
            pltpu.VMEM((2, tile * n_in, LANES), F32),
            pltpu.VMEM((2, tile * n_out, LANES), F32),
            pltpu.SemaphoreType.DMA((2,)),
            pltpu.SemaphoreType.DMA((2,)),
        ],
    )
    return pl.pallas_call(
        functools.partial(_moe_kernel, tile=tile, n_in=n_in, n_out=n_out),
        grid_spec=grid_spec,
        out_shape=jax.ShapeDtypeStruct(((n_rows + tile) * n_out, LANES), F32),
        compiler_params=_cparams("arbitrary"),
        name="moe_grouped",
    )(tile_group, n_used, src, src, dst, h2g, w1, w3, w2)


def _residual_kernel(x1_ref, y_ref, mod_ref, fg_ref, o_ref, *, final_norm):
    tm, d = x1_ref.shape
    n_out = d // LANES
    y = jnp.concatenate([y_ref[pl.ds(c, tm, stride=n_out), :] for c in range(n_out)], axis=1)
    x2 = x1_ref[...] + mod_ref[5:6, :] * y
    o_ref[...] = _rms(x2, fg_ref[...]) if final_norm else x2


def _moe_residual(x1, y_moe, mod_l, final_g, final_norm, geo):
    n_rows, d = x1.shape
    tm = geo["tm"]
    n_out = d // LANES
    mod_map = lambda i: (_mod_index(i, geo["t_lat"] // tm, geo["seq"] // tm, geo["batch"]), 0, 0)
    row_map = lambda i: (i, 0)
    return pl.pallas_call(
        functools.partial(_residual_kernel, final_norm=final_norm),
        grid=(n_rows // tm,),
        in_specs=[
            pl.BlockSpec((tm, d), row_map),
            pl.BlockSpec((tm * n_out, LANES), row_map),
            pl.BlockSpec((None, MOD_ROWS, d), mod_map),
            pl.BlockSpec((1, d), lambda i: (0, 0)),
        ],
        out_specs=pl.BlockSpec((tm, d), row_map),
        out_shape=jax.ShapeDtypeStruct((n_rows, d), F32),
        compiler_params=_cparams("parallel"),
        name="moe_residual",
    )(x1, y_moe, mod_l, final_g)


def kernel(x, c, ctx, c_ctx, w_mod, b_mod, norm1_g, norm2_g, w_in, pool_w, pool_scale, ssm_a_re, ssm_a_im, ssm_log_dt, ssm_b_re, ssm_b_im, ssm_c_re, ssm_c_im, ssm_d, glu_w, glu_b, w_branch_pool, w_branch_ssm, w_out, router_w, router_b, expert_w1, expert_w3, expert_w2, final_g):
    batch, seq, d = x.shape
    ctx_len = ctx.shape[1]
    depth = w_mod.shape[0]
    t_lat, t_ctx = batch * seq, batch * ctx_len
    assert seq % GRID_W == 0 and seq % POOL_TILE == 0 and ctx_len % SSM_CHUNK == 0
    assert t_lat % ctx_len == 0 and batch == SUBLANES
    geo = dict(batch=batch, seq=seq, ctx=ctx_len, t_lat=t_lat,
               tm=_pow2_tile(512, seq, t_ctx), tm_moe=_pow2_tile(512, seq, t_ctx))

    n_cond = 2 * SUBLANES
    c_rows = jnp.concatenate([c, c_ctx[None, :], jnp.zeros((n_cond - batch - 1, d), F32)], axis=0)
    mod = _adaln_tables(c_rows, w_mod, b_mod)

    rw = jnp.pad(router_w, ((0, 0), (0, LANES - N_EXPERTS)))
    rw_hi, rw_lo = _split_bf16(rw)
    rw_cat = jnp.concatenate([rw_hi, rw_lo], axis=1)
    rb = jnp.pad(router_b, (0, LANES - N_EXPERTS)).reshape(1, LANES)
    fg = final_g.reshape(1, d)
    d_exp = expert_w1.shape[-1]
    grouped = lambda w, a, b: w.astype(BF16).reshape(N_EXPERT_GROUPS, EXPERTS_PER_GROUP, a, b)

    x_all = jnp.concatenate([x.reshape(t_lat, d), ctx.reshape(t_ctx, d)], axis=0)
    for i in range(depth):
        last = i == depth - 1
        n_rows = t_lat if last else t_lat + t_ctx
        ops = _ssm_operators(ssm_a_re[i], ssm_a_im[i], ssm_log_dt[i], ssm_b_re[i], ssm_b_im[i],
                             ssm_c_re[i], ssm_c_im[i], ssm_d[i])
        u_pool, u4, g_pool, g_ssm = _inproj(
            x_all, mod[i], norm1_g[i].reshape(1, d), w_in[i].astype(BF16), geo)
        pooled = _pool_branch(u_pool, pool_w[i].astype(BF16), pool_scale[i], not last, geo)
        y4 = _ssm_branch(u4, ops, geo)
        wts = (glu_w[i].astype(BF16), glu_b[i].reshape(1, SSM_WIDTH), w_branch_pool[i].astype(BF16),
               w_branch_ssm[i].astype(BF16), w_out[i].astype(BF16))
        x1, h2g, grp = _merge(x_all, y4, pooled, g_pool, g_ssm, mod[i],
                              norm2_g[i].reshape(1, d), wts, (rw_cat, rb), n_rows, geo)
        y_moe = _moe(h2g, grp[0], grouped(expert_w1[i], d, d_exp), grouped(expert_w3[i], d, d_exp),
                     grouped(expert_w2[i], d_exp, d), geo)
        x_all = _moe_residual(x1, y_moe, mod[i], fg, last, geo)
    return x_all.reshape(batch, seq, d)
```

```python
import functools
import math

import numpy as np
import jax
import jax.numpy as jnp
from jax import lax
from jax.experimental import pallas as pl
from jax.experimental.pallas import tpu as pltpu

F32 = jnp.float32
BF16 = jnp.bfloat16
HIGHEST = lax.Precision.HIGHEST

GRID_W = 64
POOL_WINDOWS = (2, 4, 8, 16)
POOL_GROUP_DIM = 128
POOL_WIDTH = 512
SSM_WIDTH = 512
SSM_GROUP_DIM = 16
SSM_GROUPS = 32
SSM_STATE = 64
N_MOD = 6
N_EXPERTS = 16
EXPERTS_PER_GROUP = 4
N_EXPERT_GROUPS = 4
RMS_EPS = 1e-6

SSM_CHUNK = 16
SSM_OCTETS = SSM_GROUPS // 8
SSM_SLABS = SSM_GROUPS // 2
POOL_TILE = 256
LANES = 128
SUBLANES = 8
V7X_VMEM_LIMIT = 56 * 1024 * 1024
MOD_ROWS = 8


def _cparams(*sem):
    return pltpu.CompilerParams(dimension_semantics=sem, vmem_limit_bytes=V7X_VMEM_LIMIT)


def _pow2_tile(limit, *sizes):
    t = limit
    while any(s % t for s in sizes):
        t //= 2
    return t


def _sigmoid(v):
    return 0.5 * jnp.tanh(0.5 * v) + 0.5


def _silu(v):
    return v * _sigmoid(v)


def _gelu_tanh(v):
    c = math.sqrt(2.0 / math.pi)
    return 0.5 * v * (1.0 + jnp.tanh(c * (v + 0.044715 * (v * v * v))))


def _rms(x, g):
    ms = jnp.mean(x * x, axis=-1, keepdims=True)
    return x * lax.rsqrt(ms + RMS_EPS) * g


def _bdot(a, b):
    return jnp.dot(a, b, preferred_element_type=F32)


def _mod_kernel(c_ref, w_ref, b_ref, o_ref):
    s = _silu(c_ref[...])
    o_ref[...] = jnp.dot(s, w_ref[...], preferred_element_type=F32, precision=HIGHEST) + b_ref[...]


def _adaln_tables(c_rows, w_mod, b_mod):
    depth, d, _ = w_mod.shape
    nrow = c_rows.shape[0]
    out = pl.pallas_call(
        _mod_kernel,
        grid=(depth, N_MOD),
        in_specs=[
            pl.BlockSpec((nrow, d), lambda l, j: (0, 0)),
            pl.BlockSpec((None, d, d), lambda l, j: (l, 0, j)),
            pl.BlockSpec((None, None, 1, d), lambda l, j: (l, j, 0, 0)),
        ],
        out_specs=pl.BlockSpec((None, None, nrow, d), lambda l, j: (l, j, 0, 0)),
        out_shape=jax.ShapeDtypeStruct((depth, N_MOD, nrow, d), F32),
        compiler_params=_cparams("parallel", "parallel"),
        name="adaln_tables",
    )(c_rows, w_mod, b_mod.reshape(depth, N_MOD, 1, d))
    out = jnp.transpose(out, (0, 2, 1, 3))
    return jnp.pad(out, ((0, 0), (0, 0), (0, MOD_ROWS - N_MOD), (0, 0)))


def _inproj_kernel(x_ref, mod_ref, g_ref, w_ref, up_ref, us_ref, gp_ref, gs_ref, zs_ref):
    h = _rms(x_ref[...], g_ref[...]) * (1.0 + mod_ref[1:2, :]) + mod_ref[0:1, :]
    hb = h.astype(BF16)
    d = gp_ref.shape[1]
    o0, o1, o2 = POOL_WIDTH, POOL_WIDTH + SSM_WIDTH, POOL_WIDTH + SSM_WIDTH + d
    up_ref[...] = _bdot(hb, w_ref[:, 0:o0])
    gp_ref[...] = _bdot(hb, w_ref[:, o1:o2]).astype(BF16)
    gs_ref[...] = _bdot(hb, w_ref[:, o2:]).astype(BF16)
    zs = _bdot(hb, w_ref[:, o0:o1])
    n_chunks = zs_ref.shape[1] // SSM_CHUNK
    for j in range(zs_ref.shape[0]):
        zs_ref[j] = zs[:, j * LANES:(j + 1) * LANES]
    for t in range(SSM_CHUNK):
        for j in range(zs_ref.shape[0]):
            piece = zs_ref[j, pl.ds(t, n_chunks, stride=SSM_CHUNK), :]
            us_ref[t, :, j * LANES:(j + 1) * LANES] = piece.astype(BF16)


def _mod_index(i, lat_tiles, tiles_per_batch, n_batch):
    return jnp.where(i < lat_tiles, i // tiles_per_batch, n_batch)


def _inproj(x_all, mod_l, g1, w_in_bf, geo):
    rows, d = x_all.shape
    tm = geo["tm"]
    mod_map = lambda i: (_mod_index(i, geo["t_lat"] // tm, geo["seq"] // tm, geo["batch"]), 0, 0)
    row_map = lambda i: (i, 0)
    return pl.pallas_call(
        _inproj_kernel,
        grid=(rows // tm,),
        in_specs=[
            pl.BlockSpec((tm, d), row_map),
            pl.BlockSpec((None, MOD_ROWS, d), mod_map),
            pl.BlockSpec((1, d), lambda i: (0, 0)),
            pl.BlockSpec(w_in_bf.shape, lambda i: (0, 0)),
        ],
        out_specs=[
            pl.BlockSpec((tm, POOL_WIDTH), row_map),
            pl.BlockSpec((SSM_CHUNK, tm // SSM_CHUNK, SSM_WIDTH), lambda i: (0, i, 0)),
            pl.BlockSpec((tm, d), row_map),
            pl.BlockSpec((tm, d), row_map),
        ],
        out_shape=[
            jax.ShapeDtypeStruct((rows, POOL_WIDTH), F32),
            jax.ShapeDtypeStruct((SSM_CHUNK, rows // SSM_CHUNK, SSM_WIDTH), BF16),
            jax.ShapeDtypeStruct((rows, d), BF16),
            jax.ShapeDtypeStruct((rows, d), BF16),
        ],
        scratch_shapes=[pltpu.VMEM((SSM_WIDTH // LANES, tm, LANES), F32)],
        compiler_params=_cparams("parallel"),
        name="inproj",
    )(x_all, mod_l, g1, w_in_bf)


def _box_count(pos, w, n):
    return jnp.minimum(pos + w // 2, n) - jnp.maximum(pos - w // 2, 0)


def _split_bf16(v):
    hi = v.astype(BF16)
    lo = (v - hi.astype(F32)).astype(BF16)
    return hi, lo


def _pool_kernel(u_ref, a_ref, pw_ref, ps_ref, *rest, n_tok, grid_rows):
    o_ref = rest[0]
    pad_ref = rest[1] if grid_rows is not None else None
    gi = pl.program_id(1)
    tile = min(POOL_TILE, n_tok)
    n_tiles = n_tok // tile
    halo = (max(POOL_WINDOWS) // 2) * GRID_W

    for widx, w in enumerate(POOL_WINDOWS):

        @pl.when(gi == widx)
        def _(w=w):
            a = a_ref[...]

            def colsum(k):
                v = u_ref[pl.ds(k * tile, tile), :]
                hi, lo = _split_bf16(v)
                return v, _bdot(a, hi) + _bdot(a, lo)

            def finish(k, v, total, cnt):
                p = (total / cnt.astype(F32) - v).astype(BF16)
                y = _bdot(p, pw_ref[...]) * ps_ref[...]
                o_ref[pl.ds(k * tile, tile), :] = y.astype(o_ref.dtype)

            if grid_rows is None:
                for k in range(n_tiles):
                    v, total = colsum(k)
                    pos = lax.broadcasted_iota(jnp.int32, (tile, POOL_GROUP_DIM), 0) + k * tile
                    finish(k, v, total, _box_count(pos, w, n_tok))
            else:
                zeros = jnp.zeros((halo, POOL_GROUP_DIM), F32)
                pad_ref[pl.ds(0, halo), :] = zeros
                pad_ref[pl.ds(halo + n_tok, halo), :] = zeros
                for k in range(n_tiles):
                    _, total = colsum(k)
                    pad_ref[pl.ds(halo + k * tile, tile), :] = total
                for k in range(n_tiles):
                    acc = None
                    for sh in range(-(w // 2), w // 2):
                        part = pad_ref[pl.ds(halo + k * tile + sh * GRID_W, tile), :]
                        acc = part if acc is None else acc + part
                    tok = lax.broadcasted_iota(jnp.int32, (tile, POOL_GROUP_DIM), 0) + k * tile
                    cnt = (_box_count(tok % GRID_W, w, GRID_W)
                           * _box_count(tok // GRID_W, w, grid_rows))
                    finish(k, u_ref[pl.ds(k * tile, tile), :], acc, cnt)


def _window_matrix(w, n, period):
    t = np.arange(n)[:, None]
    s = np.arange(n)[None, :]
    inside = (s >= t - w // 2) & (s < t - w // 2 + w) & (t // period == s // period)
    return inside.astype(np.float32)


def _pool_branch(u_pool, pool_w_bf, pool_scale, need_ctx, geo):
    batch, seq, ctx_len, t_lat = geo["batch"], geo["seq"], geo["ctx"], geo["t_lat"]
    n_groups = len(POOL_WINDOWS)
    ps = pool_scale.reshape(n_groups, 1, POOL_GROUP_DIM)
    tile = min(POOL_TILE, seq)
    halo = (max(POOL_WINDOWS) // 2) * GRID_W

    def call(name, n_tok, row0, a_mat, grid_rows, scratch):
        a_dim = a_mat.shape[1]
        return pl.pallas_call(
            functools.partial(_pool_kernel, n_tok=n_tok, grid_rows=grid_rows),
            grid=(batch, n_groups),
            in_specs=[
                pl.BlockSpec((n_tok, POOL_GROUP_DIM), lambda b, g: (row0 + b, g)),
                pl.BlockSpec((None, a_dim, a_dim), lambda b, g: (g, 0, 0)),
                pl.BlockSpec((None, POOL_GROUP_DIM, POOL_GROUP_DIM), lambda b, g: (g, 0, 0)),
                pl.BlockSpec((None, 1, POOL_GROUP_DIM), lambda b, g: (g, 0, 0)),
            ],
            out_specs=pl.BlockSpec((n_tok, POOL_GROUP_DIM), lambda b, g: (b, g)),
            out_shape=jax.ShapeDtypeStruct((batch * n_tok, POOL_WIDTH), BF16),
            scratch_shapes=scratch,
            compiler_params=_cparams("parallel", "parallel"),
            name=name,
        )(u_pool, a_mat, pool_w_bf, ps)

    a_lat = jnp.asarray(np.stack([_window_matrix(w, tile, GRID_W) for w in POOL_WINDOWS]), BF16)
    p_lat = call("pool_lat", seq, 0, a_lat, seq // GRID_W,
                 [pltpu.VMEM((seq + 2 * halo, POOL_GROUP_DIM), F32)])
    if not need_ctx:
        return p_lat, None
    a_ctx = jnp.asarray(np.stack([_window_matrix(w, ctx_len, ctx_len) for w in POOL_WINDOWS]), BF16)
    return p_lat, call("pool_ctx", ctx_len, t_lat // ctx_len, a_ctx, None, [])


def _cmul(ar, ai, br, bi):
    return ar * br - ai * bi, ar * bi + ai * br


def _zoh_abar(a_re, a_im, log_dt):
    dt = jnp.exp(log_dt)
    mag = jnp.exp(a_re * dt)
    return mag * jnp.cos(a_im * dt), mag * jnp.sin(a_im * dt)


def _ssm_op_kernel(ar_ref, al_ref, bt_ref, c2_ref, cr_ref, d_ref, rep_ref,
                   m_ref, wst_ref, wout_ref, apow_ref, ak_scr, bb_scr, k_scr):
    s = pl.program_id(2)
    lc, ns = SSM_CHUNK, SSM_STATE
    n_g8 = SSM_GROUPS // SSM_OCTETS

    row = lax.broadcasted_iota(jnp.int32, (LANES, n_g8 * ns), 0) // SSM_GROUP_DIM
    col = lax.broadcasted_iota(jnp.int32, (LANES, n_g8 * ns), 1) // ns
    group_mask_wide = row == col
    row = lax.broadcasted_iota(jnp.int32, (LANES, LANES), 0)
    col = lax.broadcasted_iota(jnp.int32, (LANES, LANES), 1)
    group_mask = (row // SSM_GROUP_DIM) == (col // SSM_GROUP_DIM)
    skip = jnp.where(row == col, jnp.broadcast_to(d_ref[...], (LANES, LANES)), 0.0)

    @pl.when(s == 0)
    def _():
        for d in range(2):
            a_re, a_im, log_dt = ar_ref[d, 0], ar_ref[d, 1], ar_ref[d, 2]
            abar_re, abar_im = _zoh_abar(a_re, a_im, log_dt)
            nr, ni = abar_re - 1.0, abar_im
            den = a_re * a_re + a_im * a_im
            f_re = (nr * a_re + ni * a_im) / den
            f_im = (ni * a_re - nr * a_im) / den
            bb_re, bb_im = _cmul(f_re, f_im, bt_ref[d, 0], bt_ref[d, 1])
            bb_scr[d, 0] = bb_re
            bb_scr[d, 1] = bb_im
            p_re, p_im = jnp.ones_like(abar_re), jnp.zeros_like(abar_im)
            ak_scr[d, 0, 0] = p_re
            ak_scr[d, 0, 1] = p_im
            for k in range(1, lc + 1):
                p_re, p_im = _cmul(p_re, p_im, abar_re, abar_im)
                ak_scr[d, k, 0] = p_re
                ak_scr[d, k, 1] = p_im
            for k in range(lc):
                ab_re, ab_im = _cmul(ak_scr[d, k, 0], ak_scr[d, k, 1], bb_re, bb_im)
                full = (jnp.dot(ab_re, cr_ref[d, 0], preferred_element_type=F32, precision=HIGHEST)
                        - jnp.dot(ab_im, cr_ref[d, 1], preferred_element_type=F32, precision=HIGHEST))
                k_scr[d, k] = jnp.where(group_mask, full, 0.0)

        apow_ref[...] = jnp.zeros_like(apow_ref)
        for d in range(2):
            p_re, p_im = _zoh_abar(al_ref[d, 0:1, :], al_ref[d, 1:2, :], al_ref[d, 2:3, :])
            for _ in range(int(math.log2(lc))):
                p_re, p_im = _cmul(p_re, p_im, p_re, p_im)
            for q in range(apow_ref.shape[0]):
                apow_ref[q, 2 * d:2 * d + 1, :] = p_re[:, q * LANES:(q + 1) * LANES]
                apow_ref[q, 2 * d + 1:2 * d + 2, :] = p_im[:, q * LANES:(q + 1) * LANES]

    def widen(x):
        full = _bdot(x.astype(BF16), rep_ref[...])
        return jnp.where(group_mask_wide, full, 0.0)

    part_w = n_g8 * ns
    fwd = _cmul(ak_scr[0, lc - 1 - s, 0], ak_scr[0, lc - 1 - s, 1], bb_scr[0, 0], bb_scr[0, 1])
    bwd = _cmul(ak_scr[1, s, 0], ak_scr[1, s, 1], bb_scr[1, 0], bb_scr[1, 1])
    for p, x in enumerate(fwd + bwd):
        wst_ref[:, p * part_w:(p + 1) * part_w] = widen(x).astype(BF16)

    cf_re, cf_im = _cmul(c2_ref[0, 0], c2_ref[0, 1], ak_scr[0, s + 1, 0], ak_scr[0, s + 1, 1])
    cb_re, cb_im = _cmul(c2_ref[1, 0], c2_ref[1, 1], ak_scr[1, lc - s, 0], ak_scr[1, lc - s, 1])
    for p, x in enumerate((cf_re, -cf_im, cb_re, -cb_im)):
        wout_ref[p * part_w:(p + 1) * part_w, :] = widen(x).T.astype(BF16)

    for t in range(lc):
        fwd_blk = k_scr[0, jnp.clip(t - s, 0, lc - 1)]
        bwd_blk = k_scr[1, jnp.clip(s - t, 0, lc - 1)]
        blk = jnp.where(t > s, fwd_blk, jnp.where(t < s, bwd_blk, fwd_blk + bwd_blk + skip))
        m_ref[:, t * LANES:(t + 1) * LANES] = blk.astype(BF16)


def _ssm_operators(a_re, a_im, log_dt, b_re, b_im, c_re, c_im, d_skip):
    depth = a_re.shape[0]
    ng, ns, nj, lc = SSM_GROUPS, SSM_STATE, SSM_GROUP_DIM, SSM_CHUNK
    n_oct, n_g8 = SSM_OCTETS, SSM_GROUPS // SSM_OCTETS
    width = lc * n_g8 * nj
    slabs = SSM_SLABS // n_oct

    def rows_layout(v):
        v = jnp.broadcast_to(v.reshape(depth, 2, n_oct, n_g8, 1, ns), (depth, 2, n_oct, n_g8, nj, ns))
        return v.reshape(depth, 2, n_oct, n_g8 * nj, ns)

    def lanes_layout(v):
        return v.reshape(depth, 2, n_oct, n_g8 * ns)

    dt_gp = jnp.broadcast_to(log_dt[..., None], a_re.shape)
    a_rows = jnp.stack([rows_layout(a_re), rows_layout(a_im), rows_layout(dt_gp)], axis=3)
    a_lanes = jnp.stack([lanes_layout(a_re), lanes_layout(a_im), lanes_layout(dt_gp)], axis=3)
    b_t = jnp.stack([jnp.swapaxes(b_re, -1, -2), jnp.swapaxes(b_im, -1, -2)], axis=3)
    b_t = b_t.reshape(depth, 2, n_oct, n_g8, 2, nj, ns)
    b_t = jnp.moveaxis(b_t, 4, 3).reshape(depth, 2, n_oct, 2, n_g8 * nj, ns)
    c_2 = jnp.stack([c_re, c_im], axis=2).reshape(depth, 2, 2, n_oct, n_g8 * nj, ns)
    c_2 = jnp.moveaxis(c_2, 2, 3)
    c_r = jnp.stack([c_re, c_im], axis=2).reshape(depth, 2, 2, n_oct, n_g8 * nj, ns)
    c_r = jnp.swapaxes(jnp.moveaxis(c_r, 2, 3), -1, -2)
    d_l = d_skip.reshape(depth, n_oct, 1, n_g8 * nj)
    rep = jnp.asarray(np.tile(np.eye(ns), (1, n_g8)), BF16)

    sel = lambda *blk: pl.BlockSpec(blk, lambda l, o, s: (l, 0, o) + (0,) * (len(blk) - 3))
    big = jax.ShapeDtypeStruct((depth, n_oct, width, width), BF16)
    return pl.pallas_call(
        _ssm_op_kernel,
        grid=(depth, n_oct, lc),
        in_specs=[
            sel(None, 2, None, 3, n_g8 * nj, ns),
            sel(None, 2, None, 3, n_g8 * ns),
            sel(None, 2, None, 2, n_g8 * nj, ns),
            sel(None, 2, None, 2, n_g8 * nj, ns),
            sel(None, 2, None, 2, ns, n_g8 * nj),
            pl.BlockSpec((None, None, 1, n_g8 * nj), lambda l, o, s: (l, o, 0, 0)),
            pl.BlockSpec(rep.shape, lambda l, o, s: (0, 0)),
        ],
        out_specs=[
            pl.BlockSpec((None, None, LANES, width), lambda l, o, s: (l, o, s, 0)),
            pl.BlockSpec((None, None, LANES, width), lambda l, o, s: (l, o, s, 0)),
            pl.BlockSpec((None, None, width, LANES), lambda l, o, s: (l, o, 0, s)),
            pl.BlockSpec((None, slabs, SUBLANES, LANES), lambda l, o, s: (l, o, 0, 0)),
        ],
        out_shape=[big, big, big,
                   jax.ShapeDtypeStruct((depth, SSM_SLABS, SUBLANES, LANES), F32)],
        scratch_shapes=[
            pltpu.VMEM((2, lc + 1, 2, n_g8 * nj, ns), F32),
            pltpu.VMEM((2, 2, n_g8 * nj, ns), F32),
            pltpu.VMEM((2, lc, LANES, LANES), F32),
        ],
        compiler_params=_cparams("arbitrary", "arbitrary", "arbitrary"),
        name="ssm_operators",
    )(a_rows, a_lanes, b_t, c_2, c_r, d_l, rep)


def _chunk_rows(u_ref):
    return jnp.concatenate([u_ref[t] for t in range(SSM_CHUNK)], axis=1)


def _ssm_state_kernel(u_ref, w_ref, *s_refs):
    s = _bdot(_chunk_rows(u_ref), w_ref[...])
    part_w = s_refs[0].shape[1]
    for p, ref in enumerate(s_refs):
        ref[...] = s[:, p * part_w:(p + 1) * part_w]


def _ssm_scan_kernel(sfr_ref, sfi_ref, sbr_ref, sbi_ref, a_ref, hfr_ref, hfi_ref, hbr_ref, hbi_ref,
                     *, n_lat_chunks, n_ctx_chunks, batch):
    shape = (batch, LANES)
    afr = jnp.broadcast_to(a_ref[0:1, :], shape)
    afi = jnp.broadcast_to(a_ref[1:2, :], shape)
    abr = jnp.broadcast_to(a_ref[2:3, :], shape)
    abi = jnp.broadcast_to(a_ref[3:4, :], shape)
    zero = jnp.zeros(shape, F32)

    def chain(base, n, carry):
        def step(k, c):
            fr, fi, br, bi = c
            rf = pl.ds(base + k, batch, stride=n)
            rb = pl.ds(base + (n - 1) - k, batch, stride=n)
            hfr_ref[rf, :] = fr
            hfi_ref[rf, :] = fi
            hbr_ref[rb, :] = br
            hbi_ref[rb, :] = bi
            xr, xi = sfr_ref[rf, :], sfi_ref[rf, :]
            yr, yi = sbr_ref[rb, :], sbi_ref[rb, :]
            return (fr * afr - fi * afi + xr, fi * afr + fr * afi + xi,
                    br * abr - bi * abi + yr, bi * abr + br * abi + yi)

        return lax.fori_loop(0, n, step, carry, unroll=2)

    carry = chain(batch * n_lat_chunks, n_ctx_chunks, (zero, zero, zero, zero))
    chain(0, n_lat_chunks, carry)


def _ssm_out_kernel(u_ref, hfr_ref, hfi_ref, hbr_ref, hbi_ref, m_ref, w_ref, y_ref):
    h = jnp.concatenate([hfr_ref[...], hfi_ref[...], hbr_ref[...], hbi_ref[...]], axis=1)
    y = _bdot(_chunk_rows(u_ref), m_ref[...]) + _bdot(h.astype(BF16), w_ref[...])
    for t in range(SSM_CHUNK):
        y_ref[t] = y[:, t * LANES:(t + 1) * LANES].astype(y_ref.dtype)


def _div_tile(n, limit, mult):
    return max(t for t in range(mult, min(n, limit) + 1, mult) if n % t == 0)


def _ssm_branch(u4, ops, layer, geo):
    m_mat, w_st, w_out, a_pow = ops
    batch, seq, ctx_len = geo["batch"], geo["seq"], geo["ctx"]
    lc = SSM_CHUNK
    n_rows = u4.shape[1]
    width = m_mat.shape[-1]
    n_parts = 4
    part_w = width // n_parts
    bf16_rows = 2 * SUBLANES
    part_shape = jax.ShapeDtypeStruct((n_rows, SSM_SLABS * LANES), F32)

    def u_blk(rt):
        return pl.BlockSpec((lc, rt, LANES), lambda o, r: (0, r, o))

    def part_blk(rt):
        return pl.BlockSpec((rt, part_w), lambda o, r: (r, o))

    wmat_blk = pl.BlockSpec((None, None, width, width), lambda o, r: (layer, o, 0, 0))

    rt = _div_tile(n_rows, 544, bf16_rows)
    s_parts = pl.pallas_call(
        _ssm_state_kernel,
        grid=(SSM_OCTETS, n_rows // rt),
        in_specs=[u_blk(rt), wmat_blk],
        out_specs=[part_blk(rt)] * n_parts,
        out_shape=[part_shape] * n_parts,
        compiler_params=_cparams("parallel", "parallel"),
        name="ssm_state",
    )(u4, w_st)

    slab_blk = pl.BlockSpec((n_rows, LANES), lambda q: (0, q))
    h_parts = pl.pallas_call(
        functools.partial(_ssm_scan_kernel, n_lat_chunks=seq // lc, n_ctx_chunks=ctx_len // lc,
                          batch=batch),
        grid=(SSM_SLABS,),
        in_specs=[slab_blk] * n_parts + [pl.BlockSpec((None, None, SUBLANES, LANES),
                                                      lambda q: (layer, q, 0, 0))],
        out_specs=[slab_blk] * n_parts,
        out_shape=[part_shape] * n_parts,
        compiler_params=_cparams("parallel"),
        name="ssm_scan",
    )(*s_parts, a_pow)

    rt = _div_tile(n_rows, 272, bf16_rows)
    return pl.pallas_call(
        _ssm_out_kernel,
        grid=(SSM_OCTETS, n_rows // rt),
        in_specs=[u_blk(rt)] + [part_blk(rt)] * n_parts + [wmat_blk, wmat_blk],
        out_specs=u_blk(rt),
        out_shape=jax.ShapeDtypeStruct(u4.shape, BF16),
        compiler_params=_cparams("parallel", "parallel"),
        name="ssm_out",
    )(u4, *h_parts, m_mat, w_out)


def _route(scores_t, biased_t):
    sc = [scores_t[e:e + 1, :] for e in range(N_EXPERTS)]
    bc = [biased_t[e:e + 1, :] for e in range(N_EXPERTS)]
    npg = EXPERTS_PER_GROUP

    group_scores = []
    for g in range(N_EXPERT_GROUPS):
        v = bc[g * npg:(g + 1) * npg]
        best = None
        for i in range(npg):
            for j in range(i + 1, npg):
                pair = v[i] + v[j]
                best = pair if best is None else jnp.maximum(best, pair)
        group_scores.append(best)

    bg = jnp.zeros_like(group_scores[0], dtype=jnp.int32)
    best = group_scores[0]
    for g in range(1, N_EXPERT_GROUPS):
        upd = group_scores[g] > best
        bg = jnp.where(upd, g, bg)
        best = jnp.where(upd, group_scores[g], best)

    vb = list(bc[:npg])
    vs = list(sc[:npg])
    for g in range(1, N_EXPERT_GROUPS):
        sel = bg == g
        for j in range(npg):
            vb[j] = jnp.where(sel, bc[g * npg + j], vb[j])
            vs[j] = jnp.where(sel, sc[g * npg + j], vs[j])

    i1 = jnp.zeros_like(bg)
    m1, w1 = vb[0], vs[0]
    for j in range(1, npg):
        upd = vb[j] > m1
        i1 = jnp.where(upd, j, i1)
        m1 = jnp.where(upd, vb[j], m1)
        w1 = jnp.where(upd, vs[j], w1)

    neg = jnp.full_like(m1, -jnp.inf)
    i2 = jnp.zeros_like(bg)
    m2, w2 = neg, jnp.zeros_like(w1)
    for j in range(npg):
        cand = jnp.where(i1 == j, neg, vb[j])
        upd = cand > m2
        i2 = jnp.where(upd, j, i2)
        m2 = jnp.where(upd, cand, m2)
        w2 = jnp.where(upd, vs[j], w2)

    den = w1 + w2
    w1, w2 = w1 / den, w2 / den
    gates = [jnp.where(i1 == j, w1, 0.0) + jnp.where(i2 == j, w2, 0.0) for j in range(npg)]
    return bg, gates


def _merge_kernel(x_ref, y_ref, gp_ref, gs_ref, mod_ref, g2_ref, gluw_ref, glub_ref,
                  wbp_ref, wbs_ref, wo_ref, rw_ref, rb_ref, plat_ref, *rest, lat_tiles):
    if len(rest) == 5:
        pc_ref, x1_ref, h2_ref, grp_ref, ys_ref = rest
        pooled = jnp.where(pl.program_id(0) < lat_tiles, plat_ref[...], pc_ref[...])
    else:
        x1_ref, h2_ref, grp_ref, ys_ref = rest
        pooled = plat_ref[...]
    n_chunks = ys_ref.shape[1] // SSM_CHUNK
    n_col = ys_ref.shape[0]
    for t in range(SSM_CHUNK):
        yt = y_ref[t].astype(F32)
        for j in range(n_col):
            ys_ref[j, pl.ds(t, n_chunks, stride=SSM_CHUNK), :] = yt[:, j * LANES:(j + 1) * LANES]
    s = _gelu_tanh(jnp.concatenate([ys_ref[j] for j in range(n_col)], axis=1))
    s = s * _sigmoid(_bdot(s.astype(BF16), gluw_ref[...]) + glub_ref[...])
    m = (_sigmoid(gp_ref[...].astype(F32)) * _bdot(pooled, wbp_ref[...])
         + _sigmoid(gs_ref[...].astype(F32)) * _bdot(s.astype(BF16), wbs_ref[...]))
    x1 = x_ref[...] + mod_ref[2:3, :] * _bdot(m.astype(BF16), wo_ref[...])
    x1_ref[...] = x1
    h2 = _rms(x1, g2_ref[...]) * (1.0 + mod_ref[4:5, :]) + mod_ref[3:4, :]
    tm, d = h2.shape
    n_chunk = d // LANES + 1
    for c in range(n_chunk - 1):
        h2_ref[pl.ds(c, tm, stride=n_chunk), :] = h2[:, c * LANES:(c + 1) * LANES]

    hi, lo = _split_bf16(h2)
    r1 = _bdot(hi, rw_ref[...])
    logits = r1[:, 0:LANES] + (r1[:, LANES:] + _bdot(lo, rw_ref[:, 0:LANES]))
    scores = _sigmoid(logits)
    bg, gates = _route(scores.T, (scores + rb_ref[...]).T)
    row = lax.broadcasted_iota(jnp.int32, (SUBLANES, tm), 0)
    grp_ref[...] = jnp.broadcast_to(bg, (SUBLANES, tm))
    g8 = jnp.zeros((SUBLANES, tm), F32)
    for j, g in enumerate(gates):
        g8 = jnp.where(row == j, g, g8)
    g_t = jnp.concatenate([g8, jnp.zeros((LANES - SUBLANES, tm), F32)], axis=0)
    h2_ref[pl.ds(n_chunk - 1, tm, stride=n_chunk), :] = g_t.T


def _merge(x_all, y4, pooled, g_pool, g_ssm, mod_l, g2, wts, router, n_rows, geo):
    d = x_all.shape[1]
    tm = geo["tm"]
    n_chunk = d // LANES + 1
    lat_tiles = geo["t_lat"] // tm
    mod_map = lambda i: (_mod_index(i, lat_tiles, geo["seq"] // tm, geo["batch"]), 0, 0)
    row_map = lambda i: (i, 0)
    full = lambda a: pl.BlockSpec(a.shape, lambda i: (0,) * a.ndim)
    glu_w, glu_b, w_bp, w_bs, w_o = wts
    rw, rb = router
    p_lat, p_ctx = pooled
    pooled_specs = [pl.BlockSpec((tm, POOL_WIDTH), lambda i: (jnp.minimum(i, lat_tiles - 1), 0))]
    pooled_args = [p_lat]
    if p_ctx is not None:
        pooled_specs.append(pl.BlockSpec((tm, POOL_WIDTH), lambda i: (jnp.maximum(i - lat_tiles, 0), 0)))
        pooled_args.append(p_ctx)
    return pl.pallas_call(
        functools.partial(_merge_kernel, lat_tiles=lat_tiles),
        grid=(n_rows // tm,),
        in_specs=[
            pl.BlockSpec((tm, d), row_map),
            pl.BlockSpec((SSM_CHUNK, tm // SSM_CHUNK, SSM_WIDTH), lambda i: (0, i, 0)),
            pl.BlockSpec((tm, d), row_map),
            pl.BlockSpec((tm, d), row_map),
            pl.BlockSpec((None, MOD_ROWS, d), mod_map),
            full(g2), full(glu_w), full(glu_b), full(w_bp), full(w_bs), full(w_o), full(rw), full(rb),
        ] + pooled_specs,
        out_specs=[
            pl.BlockSpec((tm, d), row_map),
            pl.BlockSpec((tm * n_chunk, LANES), row_map),
            pl.BlockSpec((SUBLANES, tm), lambda i: (0, i)),
        ],
        out_shape=[
            jax.ShapeDtypeStruct((n_rows, d), F32),
            jax.ShapeDtypeStruct((n_rows * n_chunk, LANES), F32),
            jax.ShapeDtypeStruct((SUBLANES, n_rows), jnp.int32),
        ],
        scratch_shapes=[pltpu.VMEM((SSM_WIDTH // LANES, tm, LANES), F32)],
        compiler_params=_cparams("parallel"),
        name="merge_router",
    )(x_all, y4, g_pool, g_ssm, mod_l, g2, glu_w, glu_b, w_bp, w_bs, w_o, rw, rb, *pooled_args)


def _moe_plan(group, tile):
    n_rows = group.shape[0]
    n_tiles = n_rows // tile + N_EXPERT_GROUPS
    onehot = (group[:, None] == jnp.arange(N_EXPERT_GROUPS)[None, :]).astype(jnp.int32)
    counts = jnp.sum(onehot, axis=0)
    padded = ((counts + tile - 1) // tile) * tile
    ends = jnp.cumsum(padded)
    starts = ends - padded
    rank = jnp.sum((jnp.cumsum(onehot, axis=0) - 1) * onehot, axis=1)
    pos = jnp.sum(onehot * starts[None, :], axis=1) + rank
    token = jnp.arange(n_rows, dtype=jnp.int32)
    slot = jnp.arange(n_tiles * tile, dtype=jnp.int32)
    src = jnp.zeros((n_tiles * tile,), jnp.int32).at[pos].set(token)
    slot_group = jnp.minimum(jnp.sum(slot[:, None] >= ends[None, :], axis=1), N_EXPERT_GROUPS - 1)
    used = slot < (starts + counts)[slot_group]
    dst = jnp.where(used, src, n_rows + slot % tile)
    tile_start = jnp.arange(n_tiles, dtype=jnp.int32) * tile
    tile_group = jnp.minimum(jnp.sum(tile_start[:, None] >= ends[None, :], axis=1),
                             N_EXPERT_GROUPS - 1).astype(jnp.int32)
    n_used = (ends[-1] // tile).astype(jnp.int32).reshape(1)
    shape3 = (n_tiles, 1, tile)
    return tile_group, n_used, src.reshape(shape3), dst.reshape(shape3)


def _moe_kernel(tg_ref, nu_ref, src_ref, nxt_ref, dst_ref, h_hbm, w1_ref, w3_ref, w2_ref, y_hbm,
                hbuf, ybuf, gsem, ssem, *, tile, n_in, n_out):
    i = pl.program_id(0)
    n_used = nu_ref[0]
    slot = i % 2

    def row_copy_in(idx_ref, r, s):
        return pltpu.make_async_copy(h_hbm.at[pl.ds(idx_ref[0, r], n_in)],
                                     hbuf.at[s, pl.ds(r * n_in, n_in)], gsem.at[s])

    def row_copy_out(r, s):
        first = pl.multiple_of(dst_ref[0, r], n_out)
        return pltpu.make_async_copy(ybuf.at[s, pl.ds(pl.multiple_of(r * n_out, n_out), n_out)],
                                     y_hbm.at[pl.ds(first, n_out)], ssem.at[s])

    def tile_copy_in(s):
        return pltpu.make_async_copy(h_hbm.at[pl.ds(0, tile * n_in)], hbuf.at[s], gsem.at[s])

    def tile_copy_out(s):
        return pltpu.make_async_copy(ybuf.at[s], y_hbm.at[pl.ds(0, tile * n_out)], ssem.at[s])

    def start_rows(make):
        def body(r, carry):
            make(r).start()
            return carry
        lax.fori_loop(0, tile, body, 0, unroll=8)

    @pl.when(i == 0)
    def _():
        start_rows(lambda r: row_copy_in(src_ref, r, 0))
        ybuf[1] = jnp.zeros(ybuf.shape[1:], F32)
        spare = pltpu.make_async_copy(
            ybuf.at[1], y_hbm.at[pl.ds(y_hbm.shape[0] - tile * n_out, tile * n_out)], ssem.at[1])
        spare.start()
        spare.wait()

    @pl.when(i + 1 < n_used)
    def _():
        start_rows(lambda r: row_copy_in(nxt_ref, r, 1 - slot))

    @pl.when(i < n_used)
    def _():
        tile_copy_in(slot).wait()
        hrows, yrows = hbuf.at[slot], ybuf.at[slot]
        h = jnp.concatenate([hrows[pl.ds(c, tile, stride=n_in), :] for c in range(n_out)],
                            axis=1).astype(BF16)
        gates = hrows[pl.ds(n_in - 1, tile, stride=n_in), :]
        acc = None
        for e in range(EXPERTS_PER_GROUP):
            hid = (_silu(_bdot(h, w1_ref[e])) * _bdot(h, w3_ref[e])).astype(BF16)
            y = gates[:, e:e + 1] * _bdot(hid, w2_ref[e])
            acc = y if acc is None else acc + y

        @pl.when(i >= 2)
        def _():
            tile_copy_out(slot).wait()

        for c in range(n_out):
            yrows[pl.ds(c, tile, stride=n_out), :] = acc[:, c * LANES:(c + 1) * LANES]
        start_rows(lambda r: row_copy_out(r, slot))

    @pl.when(i == pl.num_programs(0) - 1)
    def _():
        tile_copy_out((n_used - 1) % 2).wait()

        @pl.when(n_used >= 2)
        def _():
            tile_copy_out(n_used % 2).wait()


def _moe(h2g, group, w1, w3, w2, geo):
    npg = EXPERTS_PER_GROUP
    d, d_exp = w1.shape[-2:]
    n_out = d // LANES
    n_in = n_out + 1
    n_rows = h2g.shape[0] // n_in
    tile = geo["tm_moe"]
    tile_group, n_used, src, dst = _moe_plan(group, tile)
    src, dst = src * n_in, dst * n_out
    n_tiles = tile_group.shape[0]
    idx_blk = lambda f: pl.BlockSpec((None, 1, tile), f, memory_space=pltpu.SMEM)
    w_map = lambda i, tg, nu: (tg[i], 0, 0, 0)
    grid_spec = pltpu.PrefetchScalarGridSpec(
        num_scalar_prefetch=2,
        grid=(n_tiles,),
        in_specs=[
            idx_blk(lambda i, tg, nu: (i, 0, 0)),
            idx_blk(lambda i, tg, nu: (jnp.minimum(i + 1, n_tiles - 1), 0, 0)),
            idx_blk(lambda i, tg, nu: (i, 0, 0)),
            pl.BlockSpec(memory_space=pl.ANY),
            pl.BlockSpec((None, npg, d, d_exp), w_map),
            pl.BlockSpec((None, npg, d, d_exp), w_map),
            pl.BlockSpec((None, npg, d_exp, d), w_map),
        ],
        out_specs=pl.BlockSpec(memory_space=pl.ANY),
        scratch_shapes=[
            pltpu.VMEM((2, tile * n_in, LANES), F32),
            pltpu.VMEM((2, tile * n_out, LANES), F32),
            pltpu.SemaphoreType.DMA((2,)),
            pltpu.SemaphoreType.DMA((2,)),
        ],
    )
    return pl.pallas_call(
        functools.partial(_moe_kernel, tile=tile, n_in=n_in, n_out=n_out),
        grid_spec=grid_spec,
        out_shape=jax.ShapeDtypeStruct(((n_rows + tile) * n_out, LANES), F32),
        compiler_params=_cparams("arbitrary"),
        name="moe_grouped",
    )(tile_group, n_used, src, src, dst, h2g, w1, w3, w2)


def _residual_kernel(x1_ref, y_ref, mod_ref, fg_ref, o_ref, *, final_norm):
    tm, d = x1_ref.shape
    n_out = d // LANES
    y = jnp.concatenate([y_ref[pl.ds(c, tm, stride=n_out), :] for c in range(n_out)], axis=1)
    x2 = x1_ref[...] + mod_ref[5:6, :] * y
    o_ref[...] = _rms(x2, fg_ref[...]) if final_norm else x2


def _moe_residual(x1, y_moe, mod_l, final_g, final_norm, geo):
    n_rows, d = x1.shape
    tm = geo["tm"]
    n_out = d // LANES
    mod_map = lambda i: (_mod_index(i, geo["t_lat"] // tm, geo["seq"] // tm, geo["batch"]), 0, 0)
    row_map = lambda i: (i, 0)
    return pl.pallas_call(
        functools.partial(_residual_kernel, final_norm=final_norm),
        grid=(n_rows // tm,),
        in_specs=[
            pl.BlockSpec((tm, d), row_map),
            pl.BlockSpec((tm * n_out, LANES), row_map),
            pl.BlockSpec((None, MOD_ROWS, d), mod_map),
            pl.BlockSpec((1, d), lambda i: (0, 0)),
        ],
        out_specs=pl.BlockSpec((tm, d), row_map),
        out_shape=jax.ShapeDtypeStruct((n_rows, d), F32),
        compiler_params=_cparams("parallel"),
        name="moe_residual",
    )(x1, y_moe, mod_l, final_g)


def kernel(x, c, ctx, c_ctx, w_mod, b_mod, norm1_g, norm2_g, w_in, pool_w, pool_scale, ssm_a_re, ssm_a_im, ssm_log_dt, ssm_b_re, ssm_b_im, ssm_c_re, ssm_c_im, ssm_d, glu_w, glu_b, w_branch_pool, w_branch_ssm, w_out, router_w, router_b, expert_w1, expert_w3, expert_w2, final_g):
    batch, seq, d = x.shape
    ctx_len = ctx.shape[1]
    depth = w_mod.shape[0]
    t_lat, t_ctx = batch * seq, batch * ctx_len
    assert seq % GRID_W == 0 and seq % POOL_TILE == 0 and ctx_len % SSM_CHUNK == 0
    assert t_lat % ctx_len == 0 and batch == SUBLANES
    geo = dict(batch=batch, seq=seq, ctx=ctx_len, t_lat=t_lat,
               tm=_pow2_tile(512, seq, t_ctx), tm_moe=_pow2_tile(512, seq, t_ctx))

    n_cond = 2 * SUBLANES
    c_rows = jnp.concatenate([c, c_ctx[None, :], jnp.zeros((n_cond - batch - 1, d), F32)], axis=0)
    mod = _adaln_tables(c_rows, w_mod, b_mod)

    rw = jnp.pad(router_w, ((0, 0), (0, LANES - N_EXPERTS)))
    rw_hi, rw_lo = _split_bf16(rw)
    rw_cat = jnp.concatenate([rw_hi, rw_lo], axis=1)
    rb = jnp.pad(router_b, (0, LANES - N_EXPERTS)).reshape(1, LANES)
    fg = final_g.reshape(1, d)
    d_exp = expert_w1.shape[-1]
    grouped = lambda w, a, b: w.astype(BF16).reshape(N_EXPERT_GROUPS, EXPERTS_PER_GROUP, a, b)

    ssm_ops = _ssm_operators(ssm_a_re, ssm_a_im, ssm_log_dt, ssm_b_re, ssm_b_im, ssm_c_re, ssm_c_im,
                             ssm_d)
    x_all = jnp.concatenate([x.reshape(t_lat, d), ctx.reshape(t_ctx, d)], axis=0)
    for i in range(depth):
        last = i == depth - 1
        n_rows = t_lat if last else t_lat + t_ctx
        u_pool, u4, g_pool, g_ssm = _inproj(
            x_all, mod[i], norm1_g[i].reshape(1, d), w_in[i].astype(BF16), geo)
        pooled = _pool_branch(u_pool, pool_w[i].astype(BF16), pool_scale[i], not last, geo)
        y4 = _ssm_branch(u4, ssm_ops, i, geo)
        wts = (glu_w[i].astype(BF16), glu_b[i].reshape(1, SSM_WIDTH), w_branch_pool[i].astype(BF16),
               w_branch_ssm[i].astype(BF16), w_out[i].astype(BF16))
        x1, h2g, grp = _merge(x_all, y4, pooled, g_pool, g_ssm, mod[i],
                              norm2_g[i].reshape(1, d), wts, (rw_cat, rb), n_rows, geo)
        y_moe = _moe(h2g, grp[0], grouped(expert_w1[i], d, d_exp), grouped(expert_w3[i], d, d_exp),
                     grouped(expert_w2[i], d_exp, d), geo)
        x_all = _moe_residual(x1, y_moe, mod[i], fg, last, geo)
    return x_all.reshape(batch, seq, d)
```

```python
import functools
import math

import numpy as np
import jax
import jax.numpy as jnp
from jax import lax
from jax.experimental import pallas as pl
from jax.experimental.pallas import tpu as pltpu

F32 = jnp.float32
BF16 = jnp.bfloat16
HIGHEST = lax.Precision.HIGHEST

GRID_W = 64
POOL_WINDOWS = (2, 4, 8, 16)
POOL_GROUP_DIM = 128
POOL_WIDTH = 512
SSM_WIDTH = 512
SSM_GROUP_DIM = 16
SSM_GROUPS = 32
SSM_STATE = 64
N_MOD = 6
N_EXPERTS = 16
EXPERTS_PER_GROUP = 4
N_EXPERT_GROUPS = 4
RMS_EPS = 1e-6

SSM_CHUNK = 16
SSM_OCTETS = SSM_GROUPS // 8
SSM_SLABS = SSM_GROUPS // 2
POOL_TILE = 256
LANES = 128
SUBLANES = 8
V7X_VMEM_LIMIT = 56 * 1024 * 1024
MOD_ROWS = 8


def _cparams(*sem):
    return pltpu.CompilerParams(dimension_semantics=sem, vmem_limit_bytes=V7X_VMEM_LIMIT)


def _pow2_tile(limit, *sizes):
    t = limit
    while any(s % t for s in sizes):
        t //= 2
    return t


def _sigmoid(v):
    return 0.5 * jnp.tanh(0.5 * v) + 0.5


def _silu(v):
    return v * _sigmoid(v)


def _gelu_tanh(v):
    c = math.sqrt(2.0 / math.pi)
    return 0.5 * v * (1.0 + jnp.tanh(c * (v + 0.044715 * (v * v * v))))


def _rms(x, g):
    ms = jnp.mean(x * x, axis=-1, keepdims=True)
    return x * lax.rsqrt(ms + RMS_EPS) * g


def _bdot(a, b):
    return jnp.dot(a, b, preferred_element_type=F32)


def _mod_kernel(c_ref, w_ref, b_ref, o_ref):
    s = _silu(c_ref[...])
    o_ref[...] = jnp.dot(s, w_ref[...], preferred_element_type=F32, precision=HIGHEST) + b_ref[...]


def _adaln_tables(c_rows, w_mod, b_mod):
    depth, d, _ = w_mod.shape
    nrow = c_rows.shape[0]
    out = pl.pallas_call(
        _mod_kernel,
        grid=(depth, N_MOD),
        in_specs=[
            pl.BlockSpec((nrow, d), lambda l, j: (0, 0)),
            pl.BlockSpec((None, d, d), lambda l, j: (l, 0, j)),
            pl.BlockSpec((None, None, 1, d), lambda l, j: (l, j, 0, 0)),
        ],
        out_specs=pl.BlockSpec((None, None, nrow, d), lambda l, j: (l, j, 0, 0)),
        out_shape=jax.ShapeDtypeStruct((depth, N_MOD, nrow, d), F32),
        compiler_params=_cparams("parallel", "parallel"),
        name="adaln_tables",
    )(c_rows, w_mod, b_mod.reshape(depth, N_MOD, 1, d))
    out = jnp.transpose(out, (0, 2, 1, 3))
    return jnp.pad(out, ((0, 0), (0, 0), (0, MOD_ROWS - N_MOD), (0, 0)))


def _token_rows(chunk_ref, n_tok):
    n = chunk_ref.shape[0] // n_tok
    return jnp.concatenate([chunk_ref[pl.ds(c, n_tok, stride=n), :] for c in range(n)], axis=1)


def _inproj_kernel(*refs, after_moe, lat_tiles):
    if after_moe:
        (x1_ref, y_ref, modp_ref, mod_ref, g_ref, w_ref,
         x_out_ref, up_ref, us_ref, gp_ref, gs_ref, zs_ref) = refs
        x = x1_ref[...] + modp_ref[5:6, :] * _token_rows(y_ref, x1_ref.shape[0])
    else:
        (xl_ref, xc_ref, mod_ref, g_ref, w_ref,
         x_out_ref, up_ref, us_ref, gp_ref, gs_ref, zs_ref) = refs
        x = jnp.where(pl.program_id(0) < lat_tiles, xl_ref[...], xc_ref[...])
    x_out_ref[...] = x
    h = _rms(x, g_ref[...]) * (1.0 + mod_ref[1:2, :]) + mod_ref[0:1, :]
    hb = h.astype(BF16)
    d = gp_ref.shape[1]
    o0, o1, o2 = POOL_WIDTH, POOL_WIDTH + SSM_WIDTH, POOL_WIDTH + SSM_WIDTH + d
    up_ref[...] = _bdot(hb, w_ref[:, 0:o0])
    gp_ref[...] = _bdot(hb, w_ref[:, o1:o2]).astype(BF16)
    gs_ref[...] = _bdot(hb, w_ref[:, o2:]).astype(BF16)
    zs = _bdot(hb, w_ref[:, o0:o1])
    n_chunks = zs_ref.shape[1] // SSM_CHUNK
    for j in range(zs_ref.shape[0]):
        zs_ref[j] = zs[:, j * LANES:(j + 1) * LANES]
    for t in range(SSM_CHUNK):
        for j in range(zs_ref.shape[0]):
            piece = zs_ref[j, pl.ds(t, n_chunks, stride=SSM_CHUNK), :]
            us_ref[t, :, j * LANES:(j + 1) * LANES] = piece.astype(BF16)


def _mod_index(i, lat_tiles, tiles_per_batch, n_batch):
    return jnp.where(i < lat_tiles, i // tiles_per_batch, n_batch)


def _inproj(stream, mod_l, g1, w_in_bf, geo):
    after_moe = len(stream) == 3
    d = stream[0].shape[1]
    rows = geo["t_lat"] + geo["batch"] * geo["ctx"]
    tm = geo["tm"]
    lat_tiles = geo["t_lat"] // tm
    mod_map = lambda i: (_mod_index(i, lat_tiles, geo["seq"] // tm, geo["batch"]), 0, 0)
    row_map = lambda i: (i, 0)
    row_blk = pl.BlockSpec((tm, d), row_map)
    mod_blk = pl.BlockSpec((None, MOD_ROWS, d), mod_map)
    if after_moe:
        stream_specs = [row_blk, pl.BlockSpec((tm * (d // LANES), LANES), row_map), mod_blk]
    else:
        stream_specs = [pl.BlockSpec((tm, d), lambda i: (jnp.minimum(i, lat_tiles - 1), 0)),
                        pl.BlockSpec((tm, d), lambda i: (jnp.maximum(i - lat_tiles, 0), 0))]
    return pl.pallas_call(
        functools.partial(_inproj_kernel, after_moe=after_moe, lat_tiles=lat_tiles),
        grid=(rows // tm,),
        in_specs=stream_specs + [
            mod_blk,
            pl.BlockSpec((1, d), lambda i: (0, 0)),
            pl.BlockSpec(w_in_bf.shape, lambda i: (0, 0)),
        ],
        out_specs=[
            row_blk,
            pl.BlockSpec((tm, POOL_WIDTH), row_map),
            pl.BlockSpec((SSM_CHUNK, tm // SSM_CHUNK, SSM_WIDTH), lambda i: (0, i, 0)),
            row_blk,
            row_blk,
        ],
        out_shape=[
            jax.ShapeDtypeStruct((rows, d), F32),
            jax.ShapeDtypeStruct((rows, POOL_WIDTH), F32),
            jax.ShapeDtypeStruct((SSM_CHUNK, rows // SSM_CHUNK, SSM_WIDTH), BF16),
            jax.ShapeDtypeStruct((rows, d), BF16),
            jax.ShapeDtypeStruct((rows, d), BF16),
        ],
        scratch_shapes=[pltpu.VMEM((SSM_WIDTH // LANES, tm, LANES), F32)],
        compiler_params=_cparams("parallel"),
        name="inproj",
    )(*stream, mod_l, g1, w_in_bf)


def _box_count(pos, w, n):
    return np.minimum(pos + w // 2, n) - np.maximum(pos - w // 2, 0)


def _split_bf16(v):
    hi = v.astype(BF16)
    lo = (v - hi.astype(F32)).astype(BF16)
    return hi, lo


def _pool_kernel(u_ref, a_ref, inv_ref, pw_ref, ps_ref, *rest, n_tok, grid_rows):
    o_ref = rest[0]
    pad_ref = rest[1] if grid_rows is not None else None
    gi = pl.program_id(1)
    tile = min(POOL_TILE, n_tok)
    n_tiles = n_tok // tile
    halo = (max(POOL_WINDOWS) // 2) * GRID_W

    for widx, w in enumerate(POOL_WINDOWS):

        @pl.when(gi == widx)
        def _(w=w):
            a = a_ref[...]

            def colsum(k):
                v = u_ref[pl.ds(k * tile, tile), :]
                hi, lo = _split_bf16(v)
                return v, _bdot(a, hi) + _bdot(a, lo)

            def finish(k, v, total, inv):
                p = (total * inv - v).astype(BF16)
                y = _bdot(p, pw_ref[...]) * ps_ref[...]
                o_ref[pl.ds(k * tile, tile), :] = y.astype(o_ref.dtype)

            if grid_rows is None:
                for k in range(n_tiles):
                    v, total = colsum(k)
                    finish(k, v, total, inv_ref[...])
            else:
                rows_per_tile = tile // GRID_W
                zeros = jnp.zeros((halo, POOL_GROUP_DIM), F32)
                pad_ref[pl.ds(0, halo), :] = zeros
                pad_ref[pl.ds(halo + n_tok, halo), :] = zeros
                for k in range(n_tiles):
                    _, total = colsum(k)
                    pad_ref[pl.ds(halo + k * tile, tile), :] = total
                for k in range(n_tiles):
                    acc = None
                    for sh in range(-(w // 2), w // 2):
                        part = pad_ref[pl.ds(halo + k * tile + sh * GRID_W, tile), :]
                        acc = part if acc is None else acc + part
                    inv = jnp.concatenate([
                        inv_ref[pl.ds(q * GRID_W, GRID_W), :]
                        * float(1.0 / _box_count(k * rows_per_tile + q, w, grid_rows))
                        for q in range(rows_per_tile)], axis=0)
                    finish(k, u_ref[pl.ds(k * tile, tile), :], acc, inv)


def _window_matrix(w, n, period):
    t = np.arange(n)[:, None]
    s = np.arange(n)[None, :]
    inside = (s >= t - w // 2) & (s < t - w // 2 + w) & (t // period == s // period)
    return inside.astype(np.float32)


def _pool_branch(u_pool, pool_w_bf, pool_scale, need_ctx, geo):
    batch, seq, ctx_len, t_lat = geo["batch"], geo["seq"], geo["ctx"], geo["t_lat"]
    n_groups = len(POOL_WINDOWS)
    ps = pool_scale.reshape(n_groups, 1, POOL_GROUP_DIM)
    tile = min(POOL_TILE, seq)
    halo = (max(POOL_WINDOWS) // 2) * GRID_W

    def call(name, n_tok, row0, a_dim, period, grid_rows, scratch):
        a_mat = jnp.asarray(np.stack([_window_matrix(w, a_dim, period) for w in POOL_WINDOWS]), BF16)
        inv = np.stack([1.0 / _box_count(np.arange(a_dim) % period, w, period) for w in POOL_WINDOWS])
        inv = jnp.asarray(np.broadcast_to(inv[:, :, None], inv.shape + (POOL_GROUP_DIM,)), F32)
        return pl.pallas_call(
            functools.partial(_pool_kernel, n_tok=n_tok, grid_rows=grid_rows),
            grid=(batch, n_groups),
            in_specs=[
                pl.BlockSpec((n_tok, POOL_GROUP_DIM), lambda b, g: (row0 + b, g)),
                pl.BlockSpec((None, a_dim, a_dim), lambda b, g: (g, 0, 0)),
                pl.BlockSpec((None, a_dim, POOL_GROUP_DIM), lambda b, g: (g, 0, 0)),
                pl.BlockSpec((None, POOL_GROUP_DIM, POOL_GROUP_DIM), lambda b, g: (g, 0, 0)),
                pl.BlockSpec((None, 1, POOL_GROUP_DIM), lambda b, g: (g, 0, 0)),
            ],
            out_specs=pl.BlockSpec((n_tok, POOL_GROUP_DIM), lambda b, g: (b, g)),
            out_shape=jax.ShapeDtypeStruct((batch * n_tok, POOL_WIDTH), BF16),
            scratch_shapes=scratch,
            compiler_params=_cparams("parallel", "parallel"),
            name=name,
        )(u_pool, a_mat, inv, pool_w_bf, ps)

    p_lat = call("pool_lat", seq, 0, tile, GRID_W, seq // GRID_W,
                 [pltpu.VMEM((seq + 2 * halo, POOL_GROUP_DIM), F32)])
    if not need_ctx:
        return p_lat, None
    return p_lat, call("pool_ctx", ctx_len, t_lat // ctx_len, ctx_len, ctx_len, None, [])


def _cmul(ar, ai, br, bi):
    return ar * br - ai * bi, ar * bi + ai * br


def _zoh_abar(a_re, a_im, log_dt):
    dt = jnp.exp(log_dt)
    mag = jnp.exp(a_re * dt)
    return mag * jnp.cos(a_im * dt), mag * jnp.sin(a_im * dt)


def _ssm_op_kernel(ar_ref, al_ref, bt_ref, c2_ref, cr_ref, d_ref, rep_ref,
                   m_ref, wst_ref, wout_ref, apow_ref, ak_scr, bb_scr, k_scr):
    s = pl.program_id(2)
    lc, ns = SSM_CHUNK, SSM_STATE
    n_g8 = SSM_GROUPS // SSM_OCTETS

    row = lax.broadcasted_iota(jnp.int32, (LANES, n_g8 * ns), 0) // SSM_GROUP_DIM
    col = lax.broadcasted_iota(jnp.int32, (LANES, n_g8 * ns), 1) // ns
    group_mask_wide = row == col
    row = lax.broadcasted_iota(jnp.int32, (LANES, LANES), 0)
    col = lax.broadcasted_iota(jnp.int32, (LANES, LANES), 1)
    group_mask = (row // SSM_GROUP_DIM) == (col // SSM_GROUP_DIM)
    skip = jnp.where(row == col, jnp.broadcast_to(d_ref[...], (LANES, LANES)), 0.0)

    @pl.when(s == 0)
    def _():
        for d in range(2):
            a_re, a_im, log_dt = ar_ref[d, 0], ar_ref[d, 1], ar_ref[d, 2]
            abar_re, abar_im = _zoh_abar(a_re, a_im, log_dt)
            nr, ni = abar_re - 1.0, abar_im
            den = a_re * a_re + a_im * a_im
            f_re = (nr * a_re + ni * a_im) / den
            f_im = (ni * a_re - nr * a_im) / den
            bb_re, bb_im = _cmul(f_re, f_im, bt_ref[d, 0], bt_ref[d, 1])
            bb_scr[d, 0] = bb_re
            bb_scr[d, 1] = bb_im
            p_re, p_im = jnp.ones_like(abar_re), jnp.zeros_like(abar_im)
            ak_scr[d, 0, 0] = p_re
            ak_scr[d, 0, 1] = p_im
            for k in range(1, lc + 1):
                p_re, p_im = _cmul(p_re, p_im, abar_re, abar_im)
                ak_scr[d, k, 0] = p_re
                ak_scr[d, k, 1] = p_im
            for k in range(lc):
                ab_re, ab_im = _cmul(ak_scr[d, k, 0], ak_scr[d, k, 1], bb_re, bb_im)
                full = (jnp.dot(ab_re, cr_ref[d, 0], preferred_element_type=F32, precision=HIGHEST)
                        - jnp.dot(ab_im, cr_ref[d, 1], preferred_element_type=F32, precision=HIGHEST))
                k_scr[d, k] = jnp.where(group_mask, full, 0.0)

        apow_ref[...] = jnp.zeros_like(apow_ref)
        for d in range(2):
            p_re, p_im = _zoh_abar(al_ref[d, 0:1, :], al_ref[d, 1:2, :], al_ref[d, 2:3, :])
            for _ in range(int(math.log2(lc))):
                p_re, p_im = _cmul(p_re, p_im, p_re, p_im)
            for q in range(apow_ref.shape[0]):
                apow_ref[q, 2 * d:2 * d + 1, :] = p_re[:, q * LANES:(q + 1) * LANES]
                apow_ref[q, 2 * d + 1:2 * d + 2, :] = p_im[:, q * LANES:(q + 1) * LANES]

    def widen(x):
        full = _bdot(x.astype(BF16), rep_ref[...])
        return jnp.where(group_mask_wide, full, 0.0)

    part_w = n_g8 * ns
    fwd = _cmul(ak_scr[0, lc - 1 - s, 0], ak_scr[0, lc - 1 - s, 1], bb_scr[0, 0], bb_scr[0, 1])
    bwd = _cmul(ak_scr[1, s, 0], ak_scr[1, s, 1], bb_scr[1, 0], bb_scr[1, 1])
    for p, x in enumerate(fwd + bwd):
        wst_ref[:, p * part_w:(p + 1) * part_w] = widen(x).astype(BF16)

    cf_re, cf_im = _cmul(c2_ref[0, 0], c2_ref[0, 1], ak_scr[0, s + 1, 0], ak_scr[0, s + 1, 1])
    cb_re, cb_im = _cmul(c2_ref[1, 0], c2_ref[1, 1], ak_scr[1, lc - s, 0], ak_scr[1, lc - s, 1])
    for p, x in enumerate((cf_re, -cf_im, cb_re, -cb_im)):
        wout_ref[p * part_w:(p + 1) * part_w, :] = widen(x).T.astype(BF16)

    for t in range(lc):
        fwd_blk = k_scr[0, jnp.clip(t - s, 0, lc - 1)]
        bwd_blk = k_scr[1, jnp.clip(s - t, 0, lc - 1)]
        blk = jnp.where(t > s, fwd_blk, jnp.where(t < s, bwd_blk, fwd_blk + bwd_blk + skip))
        m_ref[:, t * LANES:(t + 1) * LANES] = blk.astype(BF16)


def _ssm_operators(a_re, a_im, log_dt, b_re, b_im, c_re, c_im, d_skip):
    depth = a_re.shape[0]
    ng, ns, nj, lc = SSM_GROUPS, SSM_STATE, SSM_GROUP_DIM, SSM_CHUNK
    n_oct, n_g8 = SSM_OCTETS, SSM_GROUPS // SSM_OCTETS
    width = lc * n_g8 * nj
    slabs = SSM_SLABS // n_oct

    def rows_layout(v):
        v = jnp.broadcast_to(v.reshape(depth, 2, n_oct, n_g8, 1, ns), (depth, 2, n_oct, n_g8, nj, ns))
        return v.reshape(depth, 2, n_oct, n_g8 * nj, ns)

    def lanes_layout(v):
        return v.reshape(depth, 2, n_oct, n_g8 * ns)

    dt_gp = jnp.broadcast_to(log_dt[..., None], a_re.shape)
    a_rows = jnp.stack([rows_layout(a_re), rows_layout(a_im), rows_layout(dt_gp)], axis=3)
    a_lanes = jnp.stack([lanes_layout(a_re), lanes_layout(a_im), lanes_layout(dt_gp)], axis=3)
    b_t = jnp.stack([jnp.swapaxes(b_re, -1, -2), jnp.swapaxes(b_im, -1, -2)], axis=3)
    b_t = b_t.reshape(depth, 2, n_oct, n_g8, 2, nj, ns)
    b_t = jnp.moveaxis(b_t, 4, 3).reshape(depth, 2, n_oct, 2, n_g8 * nj, ns)
    c_2 = jnp.stack([c_re, c_im], axis=2).reshape(depth, 2, 2, n_oct, n_g8 * nj, ns)
    c_2 = jnp.moveaxis(c_2, 2, 3)
    c_r = jnp.stack([c_re, c_im], axis=2).reshape(depth, 2, 2, n_oct, n_g8 * nj, ns)
    c_r = jnp.swapaxes(jnp.moveaxis(c_r, 2, 3), -1, -2)
    d_l = d_skip.reshape(depth, n_oct, 1, n_g8 * nj)
    rep = jnp.asarray(np.tile(np.eye(ns), (1, n_g8)), BF16)

    sel = lambda *blk: pl.BlockSpec(blk, lambda l, o, s: (l, 0, o) + (0,) * (len(blk) - 3))
    big = jax.ShapeDtypeStruct((depth, n_oct, width, width), BF16)
    return pl.pallas_call(
        _ssm_op_kernel,
        grid=(depth, n_oct, lc),
        in_specs=[
            sel(None, 2, None, 3, n_g8 * nj, ns),
            sel(None, 2, None, 3, n_g8 * ns),
            sel(None, 2, None, 2, n_g8 * nj, ns),
            sel(None, 2, None, 2, n_g8 * nj, ns),
            sel(None, 2, None, 2, ns, n_g8 * nj),
            pl.BlockSpec((None, None, 1, n_g8 * nj), lambda l, o, s: (l, o, 0, 0)),
            pl.BlockSpec(rep.shape, lambda l, o, s: (0, 0)),
        ],
        out_specs=[
            pl.BlockSpec((None, None, LANES, width), lambda l, o, s: (l, o, s, 0)),
            pl.BlockSpec((None, None, LANES, width), lambda l, o, s: (l, o, s, 0)),
            pl.BlockSpec((None, None, width, LANES), lambda l, o, s: (l, o, 0, s)),
            pl.BlockSpec((None, slabs, SUBLANES, LANES), lambda l, o, s: (l, o, 0, 0)),
        ],
        out_shape=[big, big, big,
                   jax.ShapeDtypeStruct((depth, SSM_SLABS, SUBLANES, LANES), F32)],
        scratch_shapes=[
            pltpu.VMEM((2, lc + 1, 2, n_g8 * nj, ns), F32),
            pltpu.VMEM((2, 2, n_g8 * nj, ns), F32),
            pltpu.VMEM((2, lc, LANES, LANES), F32),
        ],
        compiler_params=_cparams("arbitrary", "arbitrary", "arbitrary"),
        name="ssm_operators",
    )(a_rows, a_lanes, b_t, c_2, c_r, d_l, rep)


def _chunk_rows(u_ref):
    return jnp.concatenate([u_ref[t] for t in range(SSM_CHUNK)], axis=1)


def _ssm_state_kernel(u_ref, w_ref, *s_refs):
    s = _bdot(_chunk_rows(u_ref), w_ref[...])
    part_w = s_refs[0].shape[1]
    for p, ref in enumerate(s_refs):
        ref[...] = s[:, p * part_w:(p + 1) * part_w]


def _ssm_scan_kernel(sfr_ref, sfi_ref, sbr_ref, sbi_ref, a_ref, hfr_ref, hfi_ref, hbr_ref, hbi_ref,
                     *, n_lat_chunks, n_ctx_chunks, batch):
    shape = (batch, LANES)
    afr = jnp.broadcast_to(a_ref[0:1, :], shape)
    afi = jnp.broadcast_to(a_ref[1:2, :], shape)
    abr = jnp.broadcast_to(a_ref[2:3, :], shape)
    abi = jnp.broadcast_to(a_ref[3:4, :], shape)
    zero = jnp.zeros(shape, F32)

    def chain(base, n, carry):
        def step(k, c):
            fr, fi, br, bi = c
            rf = pl.ds(base + k, batch, stride=n)
            rb = pl.ds(base + (n - 1) - k, batch, stride=n)
            hfr_ref[rf, :] = fr
            hfi_ref[rf, :] = fi
            hbr_ref[rb, :] = br
            hbi_ref[rb, :] = bi
            xr, xi = sfr_ref[rf, :], sfi_ref[rf, :]
            yr, yi = sbr_ref[rb, :], sbi_ref[rb, :]
            return (fr * afr - fi * afi + xr, fi * afr + fr * afi + xi,
                    br * abr - bi * abi + yr, bi * abr + br * abi + yi)

        return lax.fori_loop(0, n, step, carry, unroll=2)

    carry = chain(batch * n_lat_chunks, n_ctx_chunks, (zero, zero, zero, zero))
    chain(0, n_lat_chunks, carry)


def _ssm_out_kernel(u_ref, hfr_ref, hfi_ref, hbr_ref, hbi_ref, m_ref, w_ref, y_ref):
    h = jnp.concatenate([hfr_ref[...], hfi_ref[...], hbr_ref[...], hbi_ref[...]], axis=1)
    y = _bdot(_chunk_rows(u_ref), m_ref[...]) + _bdot(h.astype(BF16), w_ref[...])
    for t in range(SSM_CHUNK):
        y_ref[t] = y[:, t * LANES:(t + 1) * LANES].astype(y_ref.dtype)


def _div_tile(n, limit, mult):
    return max(t for t in range(mult, min(n, limit) + 1, mult) if n % t == 0)


def _ssm_branch(u4, ops, layer, geo):
    m_mat, w_st, w_out, a_pow = ops
    batch, seq, ctx_len = geo["batch"], geo["seq"], geo["ctx"]
    lc = SSM_CHUNK
    n_rows = u4.shape[1]
    width = m_mat.shape[-1]
    n_parts = 4
    part_w = width // n_parts
    bf16_rows = 2 * SUBLANES
    part_shape = jax.ShapeDtypeStruct((n_rows, SSM_SLABS * LANES), F32)

    def u_blk(rt):
        return pl.BlockSpec((lc, rt, LANES), lambda o, r: (0, r, o))

    def part_blk(rt):
        return pl.BlockSpec((rt, part_w), lambda o, r: (r, o))

    wmat_blk = pl.BlockSpec((None, None, width, width), lambda o, r: (layer, o, 0, 0))

    rt = _div_tile(n_rows, 544, bf16_rows)
    s_parts = pl.pallas_call(
        _ssm_state_kernel,
        grid=(SSM_OCTETS, n_rows // rt),
        in_specs=[u_blk(rt), wmat_blk],
        out_specs=[part_blk(rt)] * n_parts,
        out_shape=[part_shape] * n_parts,
        compiler_params=_cparams("parallel", "parallel"),
        name="ssm_state",
    )(u4, w_st)

    slab_blk = pl.BlockSpec((n_rows, LANES), lambda q: (0, q))
    h_parts = pl.pallas_call(
        functools.partial(_ssm_scan_kernel, n_lat_chunks=seq // lc, n_ctx_chunks=ctx_len // lc,
                          batch=batch),
        grid=(SSM_SLABS,),
        in_specs=[slab_blk] * n_parts + [pl.BlockSpec((None, None, SUBLANES, LANES),
                                                      lambda q: (layer, q, 0, 0))],
        out_specs=[slab_blk] * n_parts,
        out_shape=[part_shape] * n_parts,
        compiler_params=_cparams("parallel"),
        name="ssm_scan",
    )(*s_parts, a_pow)

    rt = _div_tile(n_rows, 272, bf16_rows)
    return pl.pallas_call(
        _ssm_out_kernel,
        grid=(SSM_OCTETS, n_rows // rt),
        in_specs=[u_blk(rt)] + [part_blk(rt)] * n_parts + [wmat_blk, wmat_blk],
        out_specs=u_blk(rt),
        out_shape=jax.ShapeDtypeStruct(u4.shape, BF16),
        compiler_params=_cparams("parallel", "parallel"),
        name="ssm_out",
    )(u4, *h_parts, m_mat, w_out)


def _route(scores_t, biased_t):
    sc = [scores_t[e:e + 1, :] for e in range(N_EXPERTS)]
    bc = [biased_t[e:e + 1, :] for e in range(N_EXPERTS)]
    npg = EXPERTS_PER_GROUP

    group_scores = []
    for g in range(N_EXPERT_GROUPS):
        v = bc[g * npg:(g + 1) * npg]
        best = None
        for i in range(npg):
            for j in range(i + 1, npg):
                pair = v[i] + v[j]
                best = pair if best is None else jnp.maximum(best, pair)
        group_scores.append(best)

    bg = jnp.zeros_like(group_scores[0], dtype=jnp.int32)
    best = group_scores[0]
    for g in range(1, N_EXPERT_GROUPS):
        upd = group_scores[g] > best
        bg = jnp.where(upd, g, bg)
        best = jnp.where(upd, group_scores[g], best)

    vb = list(bc[:npg])
    vs = list(sc[:npg])
    for g in range(1, N_EXPERT_GROUPS):
        sel = bg == g
        for j in range(npg):
            vb[j] = jnp.where(sel, bc[g * npg + j], vb[j])
            vs[j] = jnp.where(sel, sc[g * npg + j], vs[j])

    i1 = jnp.zeros_like(bg)
    m1, w1 = vb[0], vs[0]
    for j in range(1, npg):
        upd = vb[j] > m1
        i1 = jnp.where(upd, j, i1)
        m1 = jnp.where(upd, vb[j], m1)
        w1 = jnp.where(upd, vs[j], w1)

    neg = jnp.full_like(m1, -jnp.inf)
    i2 = jnp.zeros_like(bg)
    m2, w2 = neg, jnp.zeros_like(w1)
    for j in range(npg):
        cand = jnp.where(i1 == j, neg, vb[j])
        upd = cand > m2
        i2 = jnp.where(upd, j, i2)
        m2 = jnp.where(upd, cand, m2)
        w2 = jnp.where(upd, vs[j], w2)

    den = w1 + w2
    w1, w2 = w1 / den, w2 / den
    gates = [jnp.where(i1 == j, w1, 0.0) + jnp.where(i2 == j, w2, 0.0) for j in range(npg)]
    return bg, gates


def _merge_kernel(x_ref, y_ref, gp_ref, gs_ref, mod_ref, g2_ref, gluw_ref, glub_ref,
                  wbp_ref, wbs_ref, wo_ref, rw_ref, rb_ref, plat_ref, *rest, lat_tiles):
    if len(rest) == 5:
        pc_ref, x1_ref, h2_ref, grp_ref, ys_ref = rest
        pooled = jnp.where(pl.program_id(0) < lat_tiles, plat_ref[...], pc_ref[...])
    else:
        x1_ref, h2_ref, grp_ref, ys_ref = rest
        pooled = plat_ref[...]
    n_chunks = ys_ref.shape[1] // SSM_CHUNK
    n_col = ys_ref.shape[0]
    for t in range(SSM_CHUNK):
        yt = y_ref[t].astype(F32)
        for j in range(n_col):
            ys_ref[j, pl.ds(t, n_chunks, stride=SSM_CHUNK), :] = yt[:, j * LANES:(j + 1) * LANES]
    s = _gelu_tanh(jnp.concatenate([ys_ref[j] for j in range(n_col)], axis=1))
    s = s * _sigmoid(_bdot(s.astype(BF16), gluw_ref[...]) + glub_ref[...])
    m = (_sigmoid(gp_ref[...].astype(F32)) * _bdot(pooled, wbp_ref[...])
         + _sigmoid(gs_ref[...].astype(F32)) * _bdot(s.astype(BF16), wbs_ref[...]))
    x1 = x_ref[...] + mod_ref[2:3, :] * _bdot(m.astype(BF16), wo_ref[...])
    x1_ref[...] = x1
    h2 = _rms(x1, g2_ref[...]) * (1.0 + mod_ref[4:5, :]) + mod_ref[3:4, :]
    tm, d = h2.shape
    n_chunk = d // LANES + 1
    for c in range(n_chunk - 1):
        h2_ref[pl.ds(c, tm, stride=n_chunk), :] = h2[:, c * LANES:(c + 1) * LANES]

    hi, lo = _split_bf16(h2)
    r1 = _bdot(hi, rw_ref[...])
    logits = r1[:, 0:LANES] + (r1[:, LANES:] + _bdot(lo, rw_ref[:, 0:LANES]))
    scores = _sigmoid(logits)
    bg, gates = _route(scores.T, (scores + rb_ref[...]).T)
    row = lax.broadcasted_iota(jnp.int32, (SUBLANES, tm), 0)
    grp_ref[...] = jnp.broadcast_to(bg, (SUBLANES, tm))
    g8 = jnp.zeros((SUBLANES, tm), F32)
    for j, g in enumerate(gates):
        g8 = jnp.where(row == j, g, g8)
    g_t = jnp.concatenate([g8, jnp.zeros((LANES - SUBLANES, tm), F32)], axis=0)
    h2_ref[pl.ds(n_chunk - 1, tm, stride=n_chunk), :] = g_t.T


def _merge(x_all, y4, pooled, g_pool, g_ssm, mod_l, g2, wts, router, n_rows, geo):
    d = x_all.shape[1]
    tm = geo["tm"]
    n_chunk = d // LANES + 1
    lat_tiles = geo["t_lat"] // tm
    mod_map = lambda i: (_mod_index(i, lat_tiles, geo["seq"] // tm, geo["batch"]), 0, 0)
    row_map = lambda i: (i, 0)
    full = lambda a: pl.BlockSpec(a.shape, lambda i: (0,) * a.ndim)
    glu_w, glu_b, w_bp, w_bs, w_o = wts
    rw, rb = router
    p_lat, p_ctx = pooled
    pooled_specs = [pl.BlockSpec((tm, POOL_WIDTH), lambda i: (jnp.minimum(i, lat_tiles - 1), 0))]
    pooled_args = [p_lat]
    if p_ctx is not None:
        pooled_specs.append(pl.BlockSpec((tm, POOL_WIDTH), lambda i: (jnp.maximum(i - lat_tiles, 0), 0)))
        pooled_args.append(p_ctx)
    return pl.pallas_call(
        functools.partial(_merge_kernel, lat_tiles=lat_tiles),
        grid=(n_rows // tm,),
        in_specs=[
            pl.BlockSpec((tm, d), row_map),
            pl.BlockSpec((SSM_CHUNK, tm // SSM_CHUNK, SSM_WIDTH), lambda i: (0, i, 0)),
            pl.BlockSpec((tm, d), row_map),
            pl.BlockSpec((tm, d), row_map),
            pl.BlockSpec((None, MOD_ROWS, d), mod_map),
            full(g2), full(glu_w), full(glu_b), full(w_bp), full(w_bs), full(w_o), full(rw), full(rb),
        ] + pooled_specs,
        out_specs=[
            pl.BlockSpec((tm, d), row_map),
            pl.BlockSpec((tm * n_chunk, LANES), row_map),
            pl.BlockSpec((SUBLANES, tm), lambda i: (0, i)),
        ],
        out_shape=[
            jax.ShapeDtypeStruct((n_rows, d), F32),
            jax.ShapeDtypeStruct((n_rows * n_chunk, LANES), F32),
            jax.ShapeDtypeStruct((SUBLANES, n_rows), jnp.int32),
        ],
        scratch_shapes=[pltpu.VMEM((SSM_WIDTH // LANES, tm, LANES), F32)],
        compiler_params=_cparams("parallel"),
        name="merge_router",
    )(x_all, y4, g_pool, g_ssm, mod_l, g2, glu_w, glu_b, w_bp, w_bs, w_o, rw, rb, *pooled_args)


def _moe_plan(group, tile):
    n_rows = group.shape[0]
    n_tiles = n_rows // tile + N_EXPERT_GROUPS
    onehot = (group[:, None] == jnp.arange(N_EXPERT_GROUPS)[None, :]).astype(jnp.int32)
    counts = jnp.sum(onehot, axis=0)
    padded = ((counts + tile - 1) // tile) * tile
    ends = jnp.cumsum(padded)
    starts = ends - padded
    rank = jnp.sum((jnp.cumsum(onehot, axis=0) - 1) * onehot, axis=1)
    pos = jnp.sum(onehot * starts[None, :], axis=1) + rank
    token = jnp.arange(n_rows, dtype=jnp.int32)
    slot = jnp.arange(n_tiles * tile, dtype=jnp.int32)
    src = jnp.zeros((n_tiles * tile,), jnp.int32).at[pos].set(token)
    slot_group = jnp.minimum(jnp.sum(slot[:, None] >= ends[None, :], axis=1), N_EXPERT_GROUPS - 1)
    used = slot < (starts + counts)[slot_group]
    dst = jnp.where(used, src, n_rows + slot % tile)
    tile_start = jnp.arange(n_tiles, dtype=jnp.int32) * tile
    tile_group = jnp.minimum(jnp.sum(tile_start[:, None] >= ends[None, :], axis=1),
                             N_EXPERT_GROUPS - 1).astype(jnp.int32)
    n_used = (ends[-1] // tile).astype(jnp.int32).reshape(1)
    shape3 = (n_tiles, 1, tile)
    return tile_group, n_used, src.reshape(shape3), dst.reshape(shape3)


def _moe_kernel(tg_ref, nu_ref, src_ref, nxt_ref, dst_ref, h_hbm, w1_ref, w3_ref, w2_ref, y_hbm,
                hbuf, ybuf, gsem, ssem, *, tile, n_in, n_out):
    i = pl.program_id(0)
    n_used = nu_ref[0]
    slot = i % 2

    def row_copy_in(idx_ref, r, s):
        return pltpu.make_async_copy(h_hbm.at[pl.ds(idx_ref[0, r], n_in)],
                                     hbuf.at[s, pl.ds(r * n_in, n_in)], gsem.at[s])

    def row_copy_out(r, s):
        first = pl.multiple_of(dst_ref[0, r], n_out)
        return pltpu.make_async_copy(ybuf.at[s, pl.ds(pl.multiple_of(r * n_out, n_out), n_out)],
                                     y_hbm.at[pl.ds(first, n_out)], ssem.at[s])

    def tile_copy_in(s):
        return pltpu.make_async_copy(h_hbm.at[pl.ds(0, tile * n_in)], hbuf.at[s], gsem.at[s])

    def tile_copy_out(s):
        return pltpu.make_async_copy(ybuf.at[s], y_hbm.at[pl.ds(0, tile * n_out)], ssem.at[s])

    def start_rows(make):
        def body(r, carry):
            make(r).start()
            return carry
        lax.fori_loop(0, tile, body, 0, unroll=8)

    @pl.when(i == 0)
    def _():
        start_rows(lambda r: row_copy_in(src_ref, r, 0))
        ybuf[1] = jnp.zeros(ybuf.shape[1:], F32)
        spare = pltpu.make_async_copy(
            ybuf.at[1], y_hbm.at[pl.ds(y_hbm.shape[0] - tile * n_out, tile * n_out)], ssem.at[1])
        spare.start()
        spare.wait()

    @pl.when(i + 1 < n_used)
    def _():
        start_rows(lambda r: row_copy_in(nxt_ref, r, 1 - slot))

    @pl.when(i < n_used)
    def _():
        tile_copy_in(slot).wait()
        hrows, yrows = hbuf.at[slot], ybuf.at[slot]
        h = jnp.concatenate([hrows[pl.ds(c, tile, stride=n_in), :] for c in range(n_out)],
                            axis=1).astype(BF16)
        gates = hrows[pl.ds(n_in - 1, tile, stride=n_in), :]
        acc = None
        for e in range(EXPERTS_PER_GROUP):
            hid = (_silu(_bdot(h, w1_ref[e])) * _bdot(h, w3_ref[e])).astype(BF16)
            y = gates[:, e:e + 1] * _bdot(hid, w2_ref[e])
            acc = y if acc is None else acc + y

        @pl.when(i >= 2)
        def _():
            tile_copy_out(slot).wait()

        for c in range(n_out):
            yrows[pl.ds(c, tile, stride=n_out), :] = acc[:, c * LANES:(c + 1) * LANES]
        start_rows(lambda r: row_copy_out(r, slot))

    @pl.when(i == pl.num_programs(0) - 1)
    def _():
        tile_copy_out((n_used - 1) % 2).wait()

        @pl.when(n_used >= 2)
        def _():
            tile_copy_out(n_used % 2).wait()


def _moe(h2g, group, w1, w3, w2, geo):
    npg = EXPERTS_PER_GROUP
    d, d_exp = w1.shape[-2:]
    n_out = d // LANES
    n_in = n_out + 1
    n_rows = h2g.shape[0] // n_in
    tile = geo["tm_moe"]
    tile_group, n_used, src, dst = _moe_plan(group, tile)
    src, dst = src * n_in, dst * n_out
    n_tiles = tile_group.shape[0]
    idx_blk = lambda f: pl.BlockSpec((None, 1, tile), f, memory_space=pltpu.SMEM)
    w_map = lambda i, tg, nu: (tg[i], 0, 0, 0)
    grid_spec = pltpu.PrefetchScalarGridSpec(
        num_scalar_prefetch=2,
        grid=(n_tiles,),
        in_specs=[
            idx_blk(lambda i, tg, nu: (i, 0, 0)),
            idx_blk(lambda i, tg, nu: (jnp.minimum(i + 1, n_tiles - 1), 0, 0)),
            idx_blk(lambda i, tg, nu: (i, 0, 0)),
            pl.BlockSpec(memory_space=pl.ANY),
            pl.BlockSpec((None, npg, d, d_exp), w_map),
            pl.BlockSpec((None, npg, d, d_exp), w_map),
            pl.BlockSpec((None, npg, d_exp, d), w_map),
        ],
        out_specs=pl.BlockSpec(memory_space=pl.ANY),
        scratch_shapes=[
            pltpu.VMEM((2, tile * n_in, LANES), F32),
            pltpu.VMEM((2, tile * n_out, LANES), F32),
            pltpu.SemaphoreType.DMA((2,)),
            pltpu.SemaphoreType.DMA((2,)),
        ],
    )
    return pl.pallas_call(
        functools.partial(_moe_kernel, tile=tile, n_in=n_in, n_out=n_out),
        grid_spec=grid_spec,
        out_shape=jax.ShapeDtypeStruct(((n_rows + tile) * n_out, LANES), F32),
        compiler_params=_cparams("arbitrary"),
        name="moe_grouped",
    )(tile_group, n_used, src, src, dst, h2g, w1, w3, w2)


def _residual_kernel(x1_ref, y_ref, mod_ref, fg_ref, o_ref):
    x2 = x1_ref[...] + mod_ref[5:6, :] * _token_rows(y_ref, x1_ref.shape[0])
    o_ref[...] = _rms(x2, fg_ref[...])


def _moe_residual(x1, y_moe, mod_l, final_g, geo):
    n_rows, d = x1.shape
    tm = geo["tm"]
    n_out = d // LANES
    mod_map = lambda i: (_mod_index(i, geo["t_lat"] // tm, geo["seq"] // tm, geo["batch"]), 0, 0)
    row_map = lambda i: (i, 0)
    return pl.pallas_call(
        _residual_kernel,
        grid=(n_rows // tm,),
        in_specs=[
            pl.BlockSpec((tm, d), row_map),
            pl.BlockSpec((tm * n_out, LANES), row_map),
            pl.BlockSpec((None, MOD_ROWS, d), mod_map),
            pl.BlockSpec((1, d), lambda i: (0, 0)),
        ],
        out_specs=pl.BlockSpec((tm, d), row_map),
        out_shape=jax.ShapeDtypeStruct((n_rows, d), F32),
        compiler_params=_cparams("parallel"),
        name="moe_residual",
    )(x1, y_moe, mod_l, final_g)


def kernel(x, c, ctx, c_ctx, w_mod, b_mod, norm1_g, norm2_g, w_in, pool_w, pool_scale, ssm_a_re, ssm_a_im, ssm_log_dt, ssm_b_re, ssm_b_im, ssm_c_re, ssm_c_im, ssm_d, glu_w, glu_b, w_branch_pool, w_branch_ssm, w_out, router_w, router_b, expert_w1, expert_w3, expert_w2, final_g):
    batch, seq, d = x.shape
    ctx_len = ctx.shape[1]
    depth = w_mod.shape[0]
    t_lat, t_ctx = batch * seq, batch * ctx_len
    assert seq % GRID_W == 0 and seq % POOL_TILE == 0 and ctx_len % SSM_CHUNK == 0
    assert t_lat % ctx_len == 0 and batch == SUBLANES and ctx_len <= POOL_TILE
    geo = dict(batch=batch, seq=seq, ctx=ctx_len, t_lat=t_lat,
               tm=_pow2_tile(512, seq, t_ctx), tm_moe=_pow2_tile(512, seq, t_ctx))

    n_cond = 2 * SUBLANES
    c_rows = jnp.concatenate([c, c_ctx[None, :], jnp.zeros((n_cond - batch - 1, d), F32)], axis=0)
    mod = _adaln_tables(c_rows, w_mod, b_mod)

    rw = jnp.pad(router_w, ((0, 0), (0, LANES - N_EXPERTS)))
    rw_hi, rw_lo = _split_bf16(rw)
    rw_cat = jnp.concatenate([rw_hi, rw_lo], axis=1)
    rb = jnp.pad(router_b, (0, LANES - N_EXPERTS)).reshape(1, LANES)
    fg = final_g.reshape(1, d)
    d_exp = expert_w1.shape[-1]
    grouped = lambda w, a, b: w.astype(BF16).reshape(N_EXPERT_GROUPS, EXPERTS_PER_GROUP, a, b)

    ssm_ops = _ssm_operators(ssm_a_re, ssm_a_im, ssm_log_dt, ssm_b_re, ssm_b_im, ssm_c_re, ssm_c_im,
                             ssm_d)
    stream = (x.reshape(t_lat, d), ctx.reshape(t_ctx, d))
    for i in range(depth):
        last = i == depth - 1
        n_rows = t_lat if last else t_lat + t_ctx
        x_all, u_pool, u4, g_pool, g_ssm = _inproj(
            stream, mod[i], norm1_g[i].reshape(1, d), w_in[i].astype(BF16), geo)
        pooled = _pool_branch(u_pool, pool_w[i].astype(BF16), pool_scale[i], not last, geo)
        y4 = _ssm_branch(u4, ssm_ops, i, geo)
        wts = (glu_w[i].astype(BF16), glu_b[i].reshape(1, SSM_WIDTH), w_branch_pool[i].astype(BF16),
               w_branch_ssm[i].astype(BF16), w_out[i].astype(BF16))
        x1, h2g, grp = _merge(x_all, y4, pooled, g_pool, g_ssm, mod[i],
                              norm2_g[i].reshape(1, d), wts, (rw_cat, rb), n_rows, geo)
        y_moe = _moe(h2g, grp[0], grouped(expert_w1[i], d, d_exp), grouped(expert_w3[i], d, d_exp),
                     grouped(expert_w2[i], d_exp, d), geo)
        stream = (x1, y_moe, mod[i])
    return _moe_residual(x1, y_moe, mod[depth - 1], fg, geo).reshape(batch, seq, d)
```

```python
import functools
import math

import numpy as np
import jax
import jax.numpy as jnp
from jax import lax
from jax.experimental import pallas as pl
from jax.experimental.pallas import tpu as pltpu

F32 = jnp.float32
BF16 = jnp.bfloat16
HIGHEST = lax.Precision.HIGHEST

GRID_W = 64
POOL_WINDOWS = (2, 4, 8, 16)
POOL_GROUP_DIM = 128
POOL_WIDTH = 512
SSM_WIDTH = 512
SSM_GROUP_DIM = 16
SSM_GROUPS = 32
SSM_STATE = 64
N_MOD = 6
N_EXPERTS = 16
EXPERTS_PER_GROUP = 4
N_EXPERT_GROUPS = 4
RMS_EPS = 1e-6

SSM_CHUNK = 16
SSM_OCTETS = SSM_GROUPS // 8
SSM_SLABS = SSM_GROUPS // 2
SCAN_SLABS = 1
POOL_TILE = 256
LANES = 128
SUBLANES = 8
V7X_VMEM_LIMIT = 56 * 1024 * 1024
MOD_ROWS = 8


def _cparams(*sem):
    return pltpu.CompilerParams(dimension_semantics=sem, vmem_limit_bytes=V7X_VMEM_LIMIT)


def _pow2_tile(limit, *sizes):
    t = limit
    while any(s % t for s in sizes):
        t //= 2
    return t


def _sigmoid(v):
    return 0.5 * jnp.tanh(0.5 * v) + 0.5


def _silu(v):
    return v * _sigmoid(v)


def _gelu_tanh(v):
    c = math.sqrt(2.0 / math.pi)
    return 0.5 * v * (1.0 + jnp.tanh(c * (v + 0.044715 * (v * v * v))))


def _rms(x, g):
    ms = jnp.mean(x * x, axis=-1, keepdims=True)
    return x * lax.rsqrt(ms + RMS_EPS) * g


def _bdot(a, b):
    return jnp.dot(a, b, preferred_element_type=F32)


def _mod_kernel(c_ref, w_ref, b_ref, o_ref):
    s = _silu(c_ref[...])
    o_ref[...] = jnp.dot(s, w_ref[...], preferred_element_type=F32, precision=HIGHEST) + b_ref[...]


def _adaln_tables(c_rows, w_mod, b_mod):
    depth, d, _ = w_mod.shape
    nrow = c_rows.shape[0]
    out = pl.pallas_call(
        _mod_kernel,
        grid=(depth, N_MOD),
        in_specs=[
            pl.BlockSpec((nrow, d), lambda l, j: (0, 0)),
            pl.BlockSpec((None, d, d), lambda l, j: (l, 0, j)),
            pl.BlockSpec((None, None, 1, d), lambda l, j: (l, j, 0, 0)),
        ],
        out_specs=pl.BlockSpec((None, None, nrow, d), lambda l, j: (l, j, 0, 0)),
        out_shape=jax.ShapeDtypeStruct((depth, N_MOD, nrow, d), F32),
        compiler_params=_cparams("parallel", "parallel"),
        name="adaln_tables",
    )(c_rows, w_mod, b_mod.reshape(depth, N_MOD, 1, d))
    out = jnp.transpose(out, (0, 2, 1, 3))
    return jnp.pad(out, ((0, 0), (0, 0), (0, MOD_ROWS - N_MOD), (0, 0)))


def _token_rows(chunk_ref, n_tok):
    n = chunk_ref.shape[0] // n_tok
    return jnp.concatenate([chunk_ref[pl.ds(c, n_tok, stride=n), :] for c in range(n)], axis=1)


def _inproj_kernel(*refs, after_moe, lat_tiles):
    if after_moe:
        (x1_ref, y_ref, modp_ref, mod_ref, g_ref, w_ref,
         x_out_ref, up_ref, us_ref, gp_ref, gs_ref, zs_ref) = refs
        x = x1_ref[...] + modp_ref[5:6, :] * _token_rows(y_ref, x1_ref.shape[0])
    else:
        (xl_ref, xc_ref, mod_ref, g_ref, w_ref,
         x_out_ref, up_ref, us_ref, gp_ref, gs_ref, zs_ref) = refs
        x = jnp.where(pl.program_id(0) < lat_tiles, xl_ref[...], xc_ref[...])
    x_out_ref[...] = x
    h = _rms(x, g_ref[...]) * (1.0 + mod_ref[1:2, :]) + mod_ref[0:1, :]
    hb = h.astype(BF16)
    d = gp_ref.shape[1]
    o0, o1, o2 = POOL_WIDTH, POOL_WIDTH + SSM_WIDTH, POOL_WIDTH + SSM_WIDTH + d
    up_ref[...] = _bdot(hb, w_ref[:, 0:o0])
    gp_ref[...] = _bdot(hb, w_ref[:, o1:o2]).astype(BF16)
    gs_ref[...] = _bdot(hb, w_ref[:, o2:]).astype(BF16)
    zs = _bdot(hb, w_ref[:, o0:o1])
    n_chunks = zs_ref.shape[1] // SSM_CHUNK
    for j in range(zs_ref.shape[0]):
        zs_ref[j] = zs[:, j * LANES:(j + 1) * LANES]
    for t in range(SSM_CHUNK):
        for j in range(zs_ref.shape[0]):
            piece = zs_ref[j, pl.ds(t, n_chunks, stride=SSM_CHUNK), :]
            us_ref[t, :, j * LANES:(j + 1) * LANES] = piece.astype(BF16)


def _mod_index(i, lat_tiles, tiles_per_batch, n_batch):
    return jnp.where(i < lat_tiles, i // tiles_per_batch, n_batch)


def _inproj(stream, mod_l, g1, w_in_bf, geo):
    after_moe = len(stream) == 3
    d = stream[0].shape[1]
    rows = geo["t_lat"] + geo["batch"] * geo["ctx"]
    tm = geo["tm"]
    lat_tiles = geo["t_lat"] // tm
    mod_map = lambda i: (_mod_index(i, lat_tiles, geo["seq"] // tm, geo["batch"]), 0, 0)
    row_map = lambda i: (i, 0)
    row_blk = pl.BlockSpec((tm, d), row_map)
    mod_blk = pl.BlockSpec((None, MOD_ROWS, d), mod_map)
    if after_moe:
        stream_specs = [row_blk, pl.BlockSpec((tm * (d // LANES), LANES), row_map), mod_blk]
    else:
        stream_specs = [pl.BlockSpec((tm, d), lambda i: (jnp.minimum(i, lat_tiles - 1), 0)),
                        pl.BlockSpec((tm, d), lambda i: (jnp.maximum(i - lat_tiles, 0), 0))]
    return pl.pallas_call(
        functools.partial(_inproj_kernel, after_moe=after_moe, lat_tiles=lat_tiles),
        grid=(rows // tm,),
        in_specs=stream_specs + [
            mod_blk,
            pl.BlockSpec((1, d), lambda i: (0, 0)),
            pl.BlockSpec(w_in_bf.shape, lambda i: (0, 0)),
        ],
        out_specs=[
            row_blk,
            pl.BlockSpec((tm, POOL_WIDTH), row_map),
            pl.BlockSpec((SSM_CHUNK, tm // SSM_CHUNK, SSM_WIDTH), lambda i: (0, i, 0)),
            row_blk,
            row_blk,
        ],
        out_shape=[
            jax.ShapeDtypeStruct((rows, d), F32),
            jax.ShapeDtypeStruct((rows, POOL_WIDTH), F32),
            jax.ShapeDtypeStruct((SSM_CHUNK, rows // SSM_CHUNK, SSM_WIDTH), BF16),
            jax.ShapeDtypeStruct((rows, d), BF16),
            jax.ShapeDtypeStruct((rows, d), BF16),
        ],
        scratch_shapes=[pltpu.VMEM((SSM_WIDTH // LANES, tm, LANES), F32)],
        compiler_params=_cparams("parallel"),
        name="inproj",
    )(*stream, mod_l, g1, w_in_bf)


def _box_count(pos, w, n):
    return np.minimum(pos + w // 2, n) - np.maximum(pos - w // 2, 0)


def _split_bf16(v):
    hi = v.astype(BF16)
    lo = (v - hi.astype(F32)).astype(BF16)
    return hi, lo


def _pool_kernel(u_ref, a_ref, inv_ref, pw_ref, ps_ref, *rest, n_tok, grid_rows):
    o_ref = rest[0]
    pad_ref = rest[1] if grid_rows is not None else None
    gi = pl.program_id(1)
    tile = min(POOL_TILE, n_tok)
    n_tiles = n_tok // tile
    halo = (max(POOL_WINDOWS) // 2) * GRID_W

    for widx, w in enumerate(POOL_WINDOWS):

        @pl.when(gi == widx)
        def _(w=w):
            a = a_ref[...]

            def colsum(k):
                v = u_ref[pl.ds(k * tile, tile), :]
                hi, lo = _split_bf16(v)
                return v, _bdot(a, hi) + _bdot(a, lo)

            def finish(k, v, total, inv):
                p = (total * inv - v).astype(BF16)
                y = _bdot(p, pw_ref[...]) * ps_ref[...]
                o_ref[pl.ds(k * tile, tile), :] = y.astype(o_ref.dtype)

            if grid_rows is None:
                for k in range(n_tiles):
                    v, total = colsum(k)
                    finish(k, v, total, inv_ref[...])
            else:
                rows_per_tile = tile // GRID_W
                zeros = jnp.zeros((halo, POOL_GROUP_DIM), F32)
                pad_ref[pl.ds(0, halo), :] = zeros
                pad_ref[pl.ds(halo + n_tok, halo), :] = zeros
                for k in range(n_tiles):
                    _, total = colsum(k)
                    pad_ref[pl.ds(halo + k * tile, tile), :] = total
                for k in range(n_tiles):
                    acc = None
                    for sh in range(-(w // 2), w // 2):
                        part = pad_ref[pl.ds(halo + k * tile + sh * GRID_W, tile), :]
                        acc = part if acc is None else acc + part
                    inv = jnp.concatenate([
                        inv_ref[pl.ds(q * GRID_W, GRID_W), :]
                        * float(1.0 / _box_count(k * rows_per_tile + q, w, grid_rows))
                        for q in range(rows_per_tile)], axis=0)
                    finish(k, u_ref[pl.ds(k * tile, tile), :], acc, inv)


def _window_matrix(w, n, period):
    t = np.arange(n)[:, None]
    s = np.arange(n)[None, :]
    inside = (s >= t - w // 2) & (s < t - w // 2 + w) & (t // period == s // period)
    return inside.astype(np.float32)


def _pool_branch(u_pool, pool_w_bf, pool_scale, need_ctx, geo):
    batch, seq, ctx_len, t_lat = geo["batch"], geo["seq"], geo["ctx"], geo["t_lat"]
    n_groups = len(POOL_WINDOWS)
    ps = pool_scale.reshape(n_groups, 1, POOL_GROUP_DIM)
    tile = min(POOL_TILE, seq)
    halo = (max(POOL_WINDOWS) // 2) * GRID_W

    def call(name, n_tok, row0, a_dim, period, grid_rows, scratch):
        a_mat = jnp.asarray(np.stack([_window_matrix(w, a_dim, period) for w in POOL_WINDOWS]), BF16)
        inv = np.stack([1.0 / _box_count(np.arange(a_dim) % period, w, period) for w in POOL_WINDOWS])
        inv = jnp.asarray(np.broadcast_to(inv[:, :, None], inv.shape + (POOL_GROUP_DIM,)), F32)
        return pl.pallas_call(
            functools.partial(_pool_kernel, n_tok=n_tok, grid_rows=grid_rows),
            grid=(batch, n_groups),
            in_specs=[
                pl.BlockSpec((n_tok, POOL_GROUP_DIM), lambda b, g: (row0 + b, g)),
                pl.BlockSpec((None, a_dim, a_dim), lambda b, g: (g, 0, 0)),
                pl.BlockSpec((None, a_dim, POOL_GROUP_DIM), lambda b, g: (g, 0, 0)),
                pl.BlockSpec((None, POOL_GROUP_DIM, POOL_GROUP_DIM), lambda b, g: (g, 0, 0)),
                pl.BlockSpec((None, 1, POOL_GROUP_DIM), lambda b, g: (g, 0, 0)),
            ],
            out_specs=pl.BlockSpec((n_tok, POOL_GROUP_DIM), lambda b, g: (b, g)),
            out_shape=jax.ShapeDtypeStruct((batch * n_tok, POOL_WIDTH), BF16),
            scratch_shapes=scratch,
            compiler_params=_cparams("parallel", "parallel"),
            name=name,
        )(u_pool, a_mat, inv, pool_w_bf, ps)

    p_lat = call("pool_lat", seq, 0, tile, GRID_W, seq // GRID_W,
                 [pltpu.VMEM((seq + 2 * halo, POOL_GROUP_DIM), F32)])
    if not need_ctx:
        return p_lat, None
    return p_lat, call("pool_ctx", ctx_len, t_lat // ctx_len, ctx_len, ctx_len, None, [])


def _cmul(ar, ai, br, bi):
    return ar * br - ai * bi, ar * bi + ai * br


def _zoh_abar(a_re, a_im, log_dt):
    dt = jnp.exp(log_dt)
    mag = jnp.exp(a_re * dt)
    return mag * jnp.cos(a_im * dt), mag * jnp.sin(a_im * dt)


def _ssm_op_kernel(ar_ref, al_ref, bt_ref, c2_ref, cr_ref, d_ref, rep_ref,
                   m_ref, wst_ref, wout_ref, apow_ref, ak_scr, bb_scr, k_scr):
    s = pl.program_id(2)
    lc, ns = SSM_CHUNK, SSM_STATE
    n_g8 = SSM_GROUPS // SSM_OCTETS

    row = lax.broadcasted_iota(jnp.int32, (LANES, n_g8 * ns), 0) // SSM_GROUP_DIM
    col = lax.broadcasted_iota(jnp.int32, (LANES, n_g8 * ns), 1) // ns
    group_mask_wide = row == col
    row = lax.broadcasted_iota(jnp.int32, (LANES, LANES), 0)
    col = lax.broadcasted_iota(jnp.int32, (LANES, LANES), 1)
    group_mask = (row // SSM_GROUP_DIM) == (col // SSM_GROUP_DIM)
    skip = jnp.where(row == col, jnp.broadcast_to(d_ref[...], (LANES, LANES)), 0.0)

    @pl.when(s == 0)
    def _():
        for d in range(2):
            a_re, a_im, log_dt = ar_ref[d, 0], ar_ref[d, 1], ar_ref[d, 2]
            abar_re, abar_im = _zoh_abar(a_re, a_im, log_dt)
            nr, ni = abar_re - 1.0, abar_im
            den = a_re * a_re + a_im * a_im
            f_re = (nr * a_re + ni * a_im) / den
            f_im = (ni * a_re - nr * a_im) / den
            bb_re, bb_im = _cmul(f_re, f_im, bt_ref[d, 0], bt_ref[d, 1])
            bb_scr[d, 0] = bb_re
            bb_scr[d, 1] = bb_im
            p_re, p_im = jnp.ones_like(abar_re), jnp.zeros_like(abar_im)
            ak_scr[d, 0, 0] = p_re
            ak_scr[d, 0, 1] = p_im
            for k in range(1, lc + 1):
                p_re, p_im = _cmul(p_re, p_im, abar_re, abar_im)
                ak_scr[d, k, 0] = p_re
                ak_scr[d, k, 1] = p_im
            for k in range(lc):
                ab_re, ab_im = _cmul(ak_scr[d, k, 0], ak_scr[d, k, 1], bb_re, bb_im)
                full = (jnp.dot(ab_re, cr_ref[d, 0], preferred_element_type=F32, precision=HIGHEST)
                        - jnp.dot(ab_im, cr_ref[d, 1], preferred_element_type=F32, precision=HIGHEST))
                k_scr[d, k] = jnp.where(group_mask, full, 0.0)

        apow_ref[...] = jnp.zeros_like(apow_ref)
        for d in range(2):
            p_re, p_im = _zoh_abar(al_ref[d, 0:1, :], al_ref[d, 1:2, :], al_ref[d, 2:3, :])
            for _ in range(int(math.log2(lc))):
                p_re, p_im = _cmul(p_re, p_im, p_re, p_im)
            for q in range(apow_ref.shape[0]):
                apow_ref[q, 2 * d:2 * d + 1, :] = p_re[:, q * LANES:(q + 1) * LANES]
                apow_ref[q, 2 * d + 1:2 * d + 2, :] = p_im[:, q * LANES:(q + 1) * LANES]

    def widen(x):
        full = _bdot(x.astype(BF16), rep_ref[...])
        return jnp.where(group_mask_wide, full, 0.0)

    part_w = n_g8 * ns
    fwd = _cmul(ak_scr[0, lc - 1 - s, 0], ak_scr[0, lc - 1 - s, 1], bb_scr[0, 0], bb_scr[0, 1])
    bwd = _cmul(ak_scr[1, s, 0], ak_scr[1, s, 1], bb_scr[1, 0], bb_scr[1, 1])
    for p, x in enumerate(fwd + bwd):
        wst_ref[:, p * part_w:(p + 1) * part_w] = widen(x).astype(BF16)

    cf_re, cf_im = _cmul(c2_ref[0, 0], c2_ref[0, 1], ak_scr[0, s + 1, 0], ak_scr[0, s + 1, 1])
    cb_re, cb_im = _cmul(c2_ref[1, 0], c2_ref[1, 1], ak_scr[1, lc - s, 0], ak_scr[1, lc - s, 1])
    for p, x in enumerate((cf_re, -cf_im, cb_re, -cb_im)):
        wout_ref[p * part_w:(p + 1) * part_w, :] = widen(x).T.astype(BF16)

    for t in range(lc):
        fwd_blk = k_scr[0, jnp.clip(t - s, 0, lc - 1)]
        bwd_blk = k_scr[1, jnp.clip(s - t, 0, lc - 1)]
        blk = jnp.where(t > s, fwd_blk, jnp.where(t < s, bwd_blk, fwd_blk + bwd_blk + skip))
        m_ref[:, t * LANES:(t + 1) * LANES] = blk.astype(BF16)


def _ssm_operators(a_re, a_im, log_dt, b_re, b_im, c_re, c_im, d_skip):
    depth = a_re.shape[0]
    ng, ns, nj, lc = SSM_GROUPS, SSM_STATE, SSM_GROUP_DIM, SSM_CHUNK
    n_oct, n_g8 = SSM_OCTETS, SSM_GROUPS // SSM_OCTETS
    width = lc * n_g8 * nj
    slabs = SSM_SLABS // n_oct

    def rows_layout(v):
        v = jnp.broadcast_to(v.reshape(depth, 2, n_oct, n_g8, 1, ns), (depth, 2, n_oct, n_g8, nj, ns))
        return v.reshape(depth, 2, n_oct, n_g8 * nj, ns)

    def lanes_layout(v):
        return v.reshape(depth, 2, n_oct, n_g8 * ns)

    dt_gp = jnp.broadcast_to(log_dt[..., None], a_re.shape)
    a_rows = jnp.stack([rows_layout(a_re), rows_layout(a_im), rows_layout(dt_gp)], axis=3)
    a_lanes = jnp.stack([lanes_layout(a_re), lanes_layout(a_im), lanes_layout(dt_gp)], axis=3)
    b_t = jnp.stack([jnp.swapaxes(b_re, -1, -2), jnp.swapaxes(b_im, -1, -2)], axis=3)
    b_t = b_t.reshape(depth, 2, n_oct, n_g8, 2, nj, ns)
    b_t = jnp.moveaxis(b_t, 4, 3).reshape(depth, 2, n_oct, 2, n_g8 * nj, ns)
    c_2 = jnp.stack([c_re, c_im], axis=2).reshape(depth, 2, 2, n_oct, n_g8 * nj, ns)
    c_2 = jnp.moveaxis(c_2, 2, 3)
    c_r = jnp.stack([c_re, c_im], axis=2).reshape(depth, 2, 2, n_oct, n_g8 * nj, ns)
    c_r = jnp.swapaxes(jnp.moveaxis(c_r, 2, 3), -1, -2)
    d_l = d_skip.reshape(depth, n_oct, 1, n_g8 * nj)
    rep = jnp.asarray(np.tile(np.eye(ns), (1, n_g8)), BF16)

    sel = lambda *blk: pl.BlockSpec(blk, lambda l, o, s: (l, 0, o) + (0,) * (len(blk) - 3))
    big = jax.ShapeDtypeStruct((depth, n_oct, width, width), BF16)
    return pl.pallas_call(
        _ssm_op_kernel,
        grid=(depth, n_oct, lc),
        in_specs=[
            sel(None, 2, None, 3, n_g8 * nj, ns),
            sel(None, 2, None, 3, n_g8 * ns),
            sel(None, 2, None, 2, n_g8 * nj, ns),
            sel(None, 2, None, 2, n_g8 * nj, ns),
            sel(None, 2, None, 2, ns, n_g8 * nj),
            pl.BlockSpec((None, None, 1, n_g8 * nj), lambda l, o, s: (l, o, 0, 0)),
            pl.BlockSpec(rep.shape, lambda l, o, s: (0, 0)),
        ],
        out_specs=[
            pl.BlockSpec((None, None, LANES, width), lambda l, o, s: (l, o, s, 0)),
            pl.BlockSpec((None, None, LANES, width), lambda l, o, s: (l, o, s, 0)),
            pl.BlockSpec((None, None, width, LANES), lambda l, o, s: (l, o, 0, s)),
            pl.BlockSpec((None, slabs, SUBLANES, LANES), lambda l, o, s: (l, o, 0, 0)),
        ],
        out_shape=[big, big, big,
                   jax.ShapeDtypeStruct((depth, SSM_SLABS, SUBLANES, LANES), F32)],
        scratch_shapes=[
            pltpu.VMEM((2, lc + 1, 2, n_g8 * nj, ns), F32),
            pltpu.VMEM((2, 2, n_g8 * nj, ns), F32),
            pltpu.VMEM((2, lc, LANES, LANES), F32),
        ],
        compiler_params=_cparams("arbitrary", "arbitrary", "arbitrary"),
        name="ssm_operators",
    )(a_rows, a_lanes, b_t, c_2, c_r, d_l, rep)


def _chunk_rows(u_ref):
    return jnp.concatenate([u_ref[t] for t in range(SSM_CHUNK)], axis=1)


def _ssm_state_kernel(u_ref, w_ref, *s_refs):
    s = _bdot(_chunk_rows(u_ref), w_ref[...])
    n_slab = s_refs[0].shape[0]
    for p, ref in enumerate(s_refs):
        for q in range(n_slab):
            col = (p * n_slab + q) * LANES
            ref[q] = s[:, col:col + LANES]


def _ssm_scan_kernel(sfr_ref, sfi_ref, sbr_ref, sbi_ref, a_ref, hfr_ref, hfi_ref, hbr_ref, hbi_ref,
                     *, n_lat_chunks, n_ctx_chunks, batch):
    shape = (batch, LANES)
    coef = [[jnp.broadcast_to(a_ref[k, j:j + 1, :], shape) for j in range(4)]
            for k in range(SCAN_SLABS)]
    zero = jnp.zeros(shape, F32)

    def chain(base, n, carry):
        def step(i, c):
            rf = pl.ds(base + i, batch, stride=n)
            rb = pl.ds(base + (n - 1) - i, batch, stride=n)
            out = []
            for k in range(SCAN_SLABS):
                fr, fi, br, bi = c[4 * k:4 * k + 4]
                afr, afi, abr, abi = coef[k]
                hfr_ref[k, rf, :] = fr
                hfi_ref[k, rf, :] = fi
                hbr_ref[k, rb, :] = br
                hbi_ref[k, rb, :] = bi
                out += [fr * afr - fi * afi + sfr_ref[k, rf, :], fi * afr + fr * afi + sfi_ref[k, rf, :],
                        br * abr - bi * abi + sbr_ref[k, rb, :], bi * abr + br * abi + sbi_ref[k, rb, :]]
            return tuple(out)

        return lax.fori_loop(0, n, step, carry, unroll=2)

    carry = chain(batch * n_lat_chunks, n_ctx_chunks, (zero,) * (4 * SCAN_SLABS))
    chain(0, n_lat_chunks, carry)


def _ssm_out_kernel(u_ref, hfr_ref, hfi_ref, hbr_ref, hbi_ref, m_ref, w_ref, y_ref):
    h = jnp.concatenate([ref[q] for ref in (hfr_ref, hfi_ref, hbr_ref, hbi_ref)
                         for q in range(ref.shape[0])], axis=1)
    y = _bdot(_chunk_rows(u_ref), m_ref[...]) + _bdot(h.astype(BF16), w_ref[...])
    for t in range(SSM_CHUNK):
        y_ref[t] = y[:, t * LANES:(t + 1) * LANES].astype(y_ref.dtype)


def _div_tile(n, limit, mult):
    return max(t for t in range(mult, min(n, limit) + 1, mult) if n % t == 0)


def _ssm_branch(u4, ops, layer, geo):
    m_mat, w_st, w_out, a_pow = ops
    batch, seq, ctx_len = geo["batch"], geo["seq"], geo["ctx"]
    lc = SSM_CHUNK
    n_rows = u4.shape[1]
    width = m_mat.shape[-1]
    n_parts = 4
    oct_slabs = SSM_SLABS // SSM_OCTETS
    bf16_rows = 2 * SUBLANES
    part_shape = jax.ShapeDtypeStruct((SSM_SLABS, n_rows, LANES), F32)

    def u_blk(rt):
        return pl.BlockSpec((lc, rt, LANES), lambda o, r: (0, r, o))

    def part_blk(rt):
        return pl.BlockSpec((oct_slabs, rt, LANES), lambda o, r: (o, r, 0))

    wmat_blk = pl.BlockSpec((None, None, width, width), lambda o, r: (layer, o, 0, 0))

    rt = _div_tile(n_rows, 544, bf16_rows)
    s_parts = pl.pallas_call(
        _ssm_state_kernel,
        grid=(SSM_OCTETS, n_rows // rt),
        in_specs=[u_blk(rt), wmat_blk],
        out_specs=[part_blk(rt)] * n_parts,
        out_shape=[part_shape] * n_parts,
        compiler_params=_cparams("parallel", "parallel"),
        name="ssm_state",
    )(u4, w_st)

    slab_blk = pl.BlockSpec((SCAN_SLABS, n_rows, LANES), lambda q: (q, 0, 0))
    h_parts = pl.pallas_call(
        functools.partial(_ssm_scan_kernel, n_lat_chunks=seq // lc, n_ctx_chunks=ctx_len // lc,
                          batch=batch),
        grid=(SSM_SLABS // SCAN_SLABS,),
        in_specs=[slab_blk] * n_parts + [pl.BlockSpec((None, SCAN_SLABS, SUBLANES, LANES),
                                                      lambda q: (layer, q, 0, 0))],
        out_specs=[slab_blk] * n_parts,
        out_shape=[part_shape] * n_parts,
        compiler_params=_cparams("parallel"),
        name="ssm_scan",
    )(*s_parts, a_pow)

    rt = _div_tile(n_rows, 272, bf16_rows)
    return pl.pallas_call(
        _ssm_out_kernel,
        grid=(SSM_OCTETS, n_rows // rt),
        in_specs=[u_blk(rt)] + [part_blk(rt)] * n_parts + [wmat_blk, wmat_blk],
        out_specs=u_blk(rt),
        out_shape=jax.ShapeDtypeStruct(u4.shape, BF16),
        compiler_params=_cparams("parallel", "parallel"),
        name="ssm_out",
    )(u4, *h_parts, m_mat, w_out)


def _route(scores_t, biased_t):
    sc = [scores_t[e:e + 1, :] for e in range(N_EXPERTS)]
    bc = [biased_t[e:e + 1, :] for e in range(N_EXPERTS)]
    npg = EXPERTS_PER_GROUP

    group_scores = []
    for g in range(N_EXPERT_GROUPS):
        v = bc[g * npg:(g + 1) * npg]
        best = None
        for i in range(npg):
            for j in range(i + 1, npg):
                pair = v[i] + v[j]
                best = pair if best is None else jnp.maximum(best, pair)
        group_scores.append(best)

    bg = jnp.zeros_like(group_scores[0], dtype=jnp.int32)
    best = group_scores[0]
    for g in range(1, N_EXPERT_GROUPS):
        upd = group_scores[g] > best
        bg = jnp.where(upd, g, bg)
        best = jnp.where(upd, group_scores[g], best)

    vb = list(bc[:npg])
    vs = list(sc[:npg])
    for g in range(1, N_EXPERT_GROUPS):
        sel = bg == g
        for j in range(npg):
            vb[j] = jnp.where(sel, bc[g * npg + j], vb[j])
            vs[j] = jnp.where(sel, sc[g * npg + j], vs[j])

    i1 = jnp.zeros_like(bg)
    m1, w1 = vb[0], vs[0]
    for j in range(1, npg):
        upd = vb[j] > m1
        i1 = jnp.where(upd, j, i1)
        m1 = jnp.where(upd, vb[j], m1)
        w1 = jnp.where(upd, vs[j], w1)

    neg = jnp.full_like(m1, -jnp.inf)
    i2 = jnp.zeros_like(bg)
    m2, w2 = neg, jnp.zeros_like(w1)
    for j in range(npg):
        cand = jnp.where(i1 == j, neg, vb[j])
        upd = cand > m2
        i2 = jnp.where(upd, j, i2)
        m2 = jnp.where(upd, cand, m2)
        w2 = jnp.where(upd, vs[j], w2)

    den = w1 + w2
    w1, w2 = w1 / den, w2 / den
    gates = [jnp.where(i1 == j, w1, 0.0) + jnp.where(i2 == j, w2, 0.0) for j in range(npg)]
    return bg, gates


def _merge_kernel(x_ref, y_ref, gp_ref, gs_ref, mod_ref, g2_ref, gluw_ref, glub_ref,
                  wbp_ref, wbs_ref, wo_ref, rw_ref, rb_ref, plat_ref, *rest, lat_tiles):
    if len(rest) == 5:
        pc_ref, x1_ref, h2_ref, grp_ref, ys_ref = rest
        pooled = jnp.where(pl.program_id(0) < lat_tiles, plat_ref[...], pc_ref[...])
    else:
        x1_ref, h2_ref, grp_ref, ys_ref = rest
        pooled = plat_ref[...]
    n_chunks = ys_ref.shape[1] // SSM_CHUNK
    n_col = ys_ref.shape[0]
    for t in range(SSM_CHUNK):
        yt = y_ref[t].astype(F32)
        for j in range(n_col):
            ys_ref[j, pl.ds(t, n_chunks, stride=SSM_CHUNK), :] = yt[:, j * LANES:(j + 1) * LANES]
    s = _gelu_tanh(jnp.concatenate([ys_ref[j] for j in range(n_col)], axis=1))
    s = s * _sigmoid(_bdot(s.astype(BF16), gluw_ref[...]) + glub_ref[...])
    m = (_sigmoid(gp_ref[...].astype(F32)) * _bdot(pooled, wbp_ref[...])
         + _sigmoid(gs_ref[...].astype(F32)) * _bdot(s.astype(BF16), wbs_ref[...]))
    x1 = x_ref[...] + mod_ref[2:3, :] * _bdot(m.astype(BF16), wo_ref[...])
    x1_ref[...] = x1
    h2 = _rms(x1, g2_ref[...]) * (1.0 + mod_ref[4:5, :]) + mod_ref[3:4, :]
    tm, d = h2.shape
    n_chunk = d // LANES + 1
    for c in range(n_chunk - 1):
        h2_ref[pl.ds(c, tm, stride=n_chunk), :] = h2[:, c * LANES:(c + 1) * LANES]

    hi, lo = _split_bf16(h2)
    r1 = _bdot(hi, rw_ref[...])
    logits = r1[:, 0:LANES] + (r1[:, LANES:] + _bdot(lo, rw_ref[:, 0:LANES]))
    scores = _sigmoid(logits)
    bg, gates = _route(scores.T, (scores + rb_ref[...]).T)
    row = lax.broadcasted_iota(jnp.int32, (SUBLANES, tm), 0)
    grp_ref[...] = jnp.broadcast_to(bg, (SUBLANES, tm))
    g8 = jnp.zeros((SUBLANES, tm), F32)
    for j, g in enumerate(gates):
        g8 = jnp.where(row == j, g, g8)
    g_t = jnp.concatenate([g8, jnp.zeros((LANES - SUBLANES, tm), F32)], axis=0)
    h2_ref[pl.ds(n_chunk - 1, tm, stride=n_chunk), :] = g_t.T


def _merge(x_all, y4, pooled, g_pool, g_ssm, mod_l, g2, wts, router, n_rows, geo):
    d = x_all.shape[1]
    tm = geo["tm"]
    n_chunk = d // LANES + 1
    lat_tiles = geo["t_lat"] // tm
    mod_map = lambda i: (_mod_index(i, lat_tiles, geo["seq"] // tm, geo["batch"]), 0, 0)
    row_map = lambda i: (i, 0)
    full = lambda a: pl.BlockSpec(a.shape, lambda i: (0,) * a.ndim)
    glu_w, glu_b, w_bp, w_bs, w_o = wts
    rw, rb = router
    p_lat, p_ctx = pooled
    pooled_specs = [pl.BlockSpec((tm, POOL_WIDTH), lambda i: (jnp.minimum(i, lat_tiles - 1), 0))]
    pooled_args = [p_lat]
    if p_ctx is not None:
        pooled_specs.append(pl.BlockSpec((tm, POOL_WIDTH), lambda i: (jnp.maximum(i - lat_tiles, 0), 0)))
        pooled_args.append(p_ctx)
    return pl.pallas_call(
        functools.partial(_merge_kernel, lat_tiles=lat_tiles),
        grid=(n_rows // tm,),
        in_specs=[
            pl.BlockSpec((tm, d), row_map),
            pl.BlockSpec((SSM_CHUNK, tm // SSM_CHUNK, SSM_WIDTH), lambda i: (0, i, 0)),
            pl.BlockSpec((tm, d), row_map),
            pl.BlockSpec((tm, d), row_map),
            pl.BlockSpec((None, MOD_ROWS, d), mod_map),
            full(g2), full(glu_w), full(glu_b), full(w_bp), full(w_bs), full(w_o), full(rw), full(rb),
        ] + pooled_specs,
        out_specs=[
            pl.BlockSpec((tm, d), row_map),
            pl.BlockSpec((tm * n_chunk, LANES), row_map),
            pl.BlockSpec((SUBLANES, tm), lambda i: (0, i)),
        ],
        out_shape=[
            jax.ShapeDtypeStruct((n_rows, d), F32),
            jax.ShapeDtypeStruct((n_rows * n_chunk, LANES), F32),
            jax.ShapeDtypeStruct((SUBLANES, n_rows), jnp.int32),
        ],
        scratch_shapes=[pltpu.VMEM((SSM_WIDTH // LANES, tm, LANES), F32)],
        compiler_params=_cparams("parallel"),
        name="merge_router",
    )(x_all, y4, g_pool, g_ssm, mod_l, g2, glu_w, glu_b, w_bp, w_bs, w_o, rw, rb, *pooled_args)


def _moe_plan(group, tile):
    n_rows = group.shape[0]
    n_tiles = n_rows // tile + N_EXPERT_GROUPS
    onehot = (group[:, None] == jnp.arange(N_EXPERT_GROUPS)[None, :]).astype(jnp.int32)
    counts = jnp.sum(onehot, axis=0)
    padded = ((counts + tile - 1) // tile) * tile
    ends = jnp.cumsum(padded)
    starts = ends - padded
    rank = jnp.sum((jnp.cumsum(onehot, axis=0) - 1) * onehot, axis=1)
    pos = jnp.sum(onehot * starts[None, :], axis=1) + rank
    token = jnp.arange(n_rows, dtype=jnp.int32)
    slot = jnp.arange(n_tiles * tile, dtype=jnp.int32)
    src = jnp.zeros((n_tiles * tile,), jnp.int32).at[pos].set(
        token, unique_indices=True, mode="promise_in_bounds")
    slot_group = jnp.minimum(jnp.sum(slot[:, None] >= ends[None, :], axis=1), N_EXPERT_GROUPS - 1)
    used = slot < (starts + counts)[slot_group]
    dst = jnp.where(used, src, n_rows + slot % tile)
    tile_start = jnp.arange(n_tiles, dtype=jnp.int32) * tile
    tile_group = jnp.minimum(jnp.sum(tile_start[:, None] >= ends[None, :], axis=1),
                             N_EXPERT_GROUPS - 1).astype(jnp.int32)
    n_used = (ends[-1] // tile).astype(jnp.int32).reshape(1)
    shape3 = (n_tiles, 1, tile)
    return tile_group, n_used, src.reshape(shape3), dst.reshape(shape3)


def _moe_kernel(tg_ref, nu_ref, src_ref, nxt_ref, dst_ref, h_hbm, w1_ref, w3_ref, w2_ref, y_hbm,
                hbuf, ybuf, gsem, ssem, *, tile, n_in, n_out):
    i = pl.program_id(0)
    n_used = nu_ref[0]
    slot = i % 2

    def row_copy_in(idx_ref, r, s):
        return pltpu.make_async_copy(h_hbm.at[pl.ds(idx_ref[0, r], n_in)],
                                     hbuf.at[s, pl.ds(r * n_in, n_in)], gsem.at[s])

    def row_copy_out(r, s):
        first = pl.multiple_of(dst_ref[0, r], n_out)
        return pltpu.make_async_copy(ybuf.at[s, pl.ds(pl.multiple_of(r * n_out, n_out), n_out)],
                                     y_hbm.at[pl.ds(first, n_out)], ssem.at[s])

    def tile_copy_in(s):
        return pltpu.make_async_copy(h_hbm.at[pl.ds(0, tile * n_in)], hbuf.at[s], gsem.at[s])

    def tile_copy_out(s):
        return pltpu.make_async_copy(ybuf.at[s], y_hbm.at[pl.ds(0, tile * n_out)], ssem.at[s])

    def start_rows(make):
        def body(r, carry):
            make(r).start()
            return carry
        lax.fori_loop(0, tile, body, 0, unroll=16)

    @pl.when(i == 0)
    def _():
        start_rows(lambda r: row_copy_in(src_ref, r, 0))
        ybuf[1] = jnp.zeros(ybuf.shape[1:], F32)
        spare = pltpu.make_async_copy(
            ybuf.at[1], y_hbm.at[pl.ds(y_hbm.shape[0] - tile * n_out, tile * n_out)], ssem.at[1])
        spare.start()
        spare.wait()

    @pl.when(i + 1 < n_used)
    def _():
        start_rows(lambda r: row_copy_in(nxt_ref, r, 1 - slot))

    @pl.when(i < n_used)
    def _():
        tile_copy_in(slot).wait()
        hrows, yrows = hbuf.at[slot], ybuf.at[slot]
        h = jnp.concatenate([hrows[pl.ds(c, tile, stride=n_in), :] for c in range(n_out)],
                            axis=1).astype(BF16)
        gates = hrows[pl.ds(n_in - 1, tile, stride=n_in), :]
        acc = None
        for e in range(EXPERTS_PER_GROUP):
            hid = (_silu(_bdot(h, w1_ref[e])) * _bdot(h, w3_ref[e])).astype(BF16)
            y = gates[:, e:e + 1] * _bdot(hid, w2_ref[e])
            acc = y if acc is None else acc + y

        @pl.when(i >= 2)
        def _():
            tile_copy_out(slot).wait()

        for c in range(n_out):
            yrows[pl.ds(c, tile, stride=n_out), :] = acc[:, c * LANES:(c + 1) * LANES]
        start_rows(lambda r: row_copy_out(r, slot))

    @pl.when(i == pl.num_programs(0) - 1)
    def _():
        tile_copy_out((n_used - 1) % 2).wait()

        @pl.when(n_used >= 2)
        def _():
            tile_copy_out(n_used % 2).wait()


def _moe(h2g, group, w1, w3, w2, geo):
    npg = EXPERTS_PER_GROUP
    d, d_exp = w1.shape[-2:]
    n_out = d // LANES
    n_in = n_out + 1
    n_rows = h2g.shape[0] // n_in
    tile = geo["tm_moe"]
    tile_group, n_used, src, dst = _moe_plan(group, tile)
    src, dst = src * n_in, dst * n_out
    n_tiles = tile_group.shape[0]
    idx_blk = lambda f: pl.BlockSpec((None, 1, tile), f, memory_space=pltpu.SMEM)
    w_map = lambda i, tg, nu: (tg[i], 0, 0, 0)
    grid_spec = pltpu.PrefetchScalarGridSpec(
        num_scalar_prefetch=2,
        grid=(n_tiles,),
        in_specs=[
            idx_blk(lambda i, tg, nu: (i, 0, 0)),
            idx_blk(lambda i, tg, nu: (jnp.minimum(i + 1, n_tiles - 1), 0, 0)),
            idx_blk(lambda i, tg, nu: (i, 0, 0)),
            pl.BlockSpec(memory_space=pl.ANY),
            pl.BlockSpec((None, npg, d, d_exp), w_map),
            pl.BlockSpec((None, npg, d, d_exp), w_map),
            pl.BlockSpec((None, npg, d_exp, d), w_map),
        ],
        out_specs=pl.BlockSpec(memory_space=pl.ANY),
        scratch_shapes=[
            pltpu.VMEM((2, tile * n_in, LANES), F32),
            pltpu.VMEM((2, tile * n_out, LANES), F32),
            pltpu.SemaphoreType.DMA((2,)),
            pltpu.SemaphoreType.DMA((2,)),
        ],
    )
    return pl.pallas_call(
        functools.partial(_moe_kernel, tile=tile, n_in=n_in, n_out=n_out),
        grid_spec=grid_spec,
        out_shape=jax.ShapeDtypeStruct(((n_rows + tile) * n_out, LANES), F32),
        compiler_params=_cparams("arbitrary"),
        name="moe_grouped",
    )(tile_group, n_used, src, src, dst, h2g, w1, w3, w2)


def _residual_kernel(x1_ref, y_ref, mod_ref, fg_ref, o_ref):
    x2 = x1_ref[...] + mod_ref[5:6, :] * _token_rows(y_ref, x1_ref.shape[0])
    o_ref[...] = _rms(x2, fg_ref[...])


def _moe_residual(x1, y_moe, mod_l, final_g, geo):
    n_rows, d = x1.shape
    tm = geo["tm"]
    n_out = d // LANES
    mod_map = lambda i: (_mod_index(i, geo["t_lat"] // tm, geo["seq"] // tm, geo["batch"]), 0, 0)
    row_map = lambda i: (i, 0)
    return pl.pallas_call(
        _residual_kernel,
        grid=(n_rows // tm,),
        in_specs=[
            pl.BlockSpec((tm, d), row_map),
            pl.BlockSpec((tm * n_out, LANES), row_map),
            pl.BlockSpec((None, MOD_ROWS, d), mod_map),
            pl.BlockSpec((1, d), lambda i: (0, 0)),
        ],
        out_specs=pl.BlockSpec((tm, d), row_map),
        out_shape=jax.ShapeDtypeStruct((n_rows, d), F32),
        compiler_params=_cparams("parallel"),
        name="moe_residual",
    )(x1, y_moe, mod_l, final_g)


def kernel(x, c, ctx, c_ctx, w_mod, b_mod, norm1_g, norm2_g, w_in, pool_w, pool_scale, ssm_a_re, ssm_a_im, ssm_log_dt, ssm_b_re, ssm_b_im, ssm_c_re, ssm_c_im, ssm_d, glu_w, glu_b, w_branch_pool, w_branch_ssm, w_out, router_w, router_b, expert_w1, expert_w3, expert_w2, final_g):
    batch, seq, d = x.shape
    ctx_len = ctx.shape[1]
    depth = w_mod.shape[0]
    t_lat, t_ctx = batch * seq, batch * ctx_len
    assert seq % GRID_W == 0 and seq % POOL_TILE == 0 and ctx_len % SSM_CHUNK == 0
    assert t_lat % ctx_len == 0 and batch == SUBLANES and ctx_len <= POOL_TILE
    geo = dict(batch=batch, seq=seq, ctx=ctx_len, t_lat=t_lat,
               tm=_pow2_tile(512, seq, t_ctx), tm_moe=_pow2_tile(512, seq, t_ctx))

    n_cond = 2 * SUBLANES
    c_rows = jnp.concatenate([c, c_ctx[None, :], jnp.zeros((n_cond - batch - 1, d), F32)], axis=0)
    mod = _adaln_tables(c_rows, w_mod, b_mod)

    rw = jnp.pad(router_w, ((0, 0), (0, LANES - N_EXPERTS)))
    rw_hi, rw_lo = _split_bf16(rw)
    rw_cat = jnp.concatenate([rw_hi, rw_lo], axis=1)
    rb = jnp.pad(router_b, (0, LANES - N_EXPERTS)).reshape(1, LANES)
    fg = final_g.reshape(1, d)
    d_exp = expert_w1.shape[-1]
    grouped = lambda w, a, b: w.astype(BF16).reshape(N_EXPERT_GROUPS, EXPERTS_PER_GROUP, a, b)

    ssm_ops = _ssm_operators(ssm_a_re, ssm_a_im, ssm_log_dt, ssm_b_re, ssm_b_im, ssm_c_re, ssm_c_im,
                             ssm_d)
    stream = (x.reshape(t_lat, d), ctx.reshape(t_ctx, d))
    for i in range(depth):
        last = i == depth - 1
        n_rows = t_lat if last else t_lat + t_ctx
        x_all, u_pool, u4, g_pool, g_ssm = _inproj(
            stream, mod[i], norm1_g[i].reshape(1, d), w_in[i].astype(BF16), geo)
        pooled = _pool_branch(u_pool, pool_w[i].astype(BF16), pool_scale[i], not last, geo)
        y4 = _ssm_branch(u4, ssm_ops, i, geo)
        wts = (glu_w[i].astype(BF16), glu_b[i].reshape(1, SSM_WIDTH), w_branch_pool[i].astype(BF16),
               w_branch_ssm[i].astype(BF16), w_out[i].astype(BF16))
        x1, h2g, grp = _merge(x_all, y4, pooled, g_pool, g_ssm, mod[i],
                              norm2_g[i].reshape(1, d), wts, (rw_cat, rb), n_rows, geo)
        y_moe = _moe(h2g, grp[0], grouped(expert_w1[i], d, d_exp), grouped(expert_w3[i], d, d_exp),
                     grouped(expert_w2[i], d_exp, d), geo)
        stream = (x1, y_moe, mod[i])
    return _moe_residual(x1, y_moe, mod[depth - 1], fg, geo).reshape(batch, seq, d)
```

```python
import functools
import math

import numpy as np
import jax
import jax.numpy as jnp
from jax import lax
from jax.experimental import pallas as pl
from jax.experimental.pallas import tpu as pltpu

F32 = jnp.float32
BF16 = jnp.bfloat16
HIGHEST = lax.Precision.HIGHEST

GRID_W = 64
POOL_WINDOWS = (2, 4, 8, 16)
POOL_GROUP_DIM = 128
POOL_WIDTH = 512
SSM_WIDTH = 512
SSM_GROUP_DIM = 16
SSM_GROUPS = 32
SSM_STATE = 64
N_MOD = 6
N_EXPERTS = 16
EXPERTS_PER_GROUP = 4
N_EXPERT_GROUPS = 4
RMS_EPS = 1e-6

SSM_CHUNK = 16
SSM_OCTETS = SSM_GROUPS // 8
SSM_SLABS = SSM_GROUPS // 2
SCAN_SLABS = 1
OP_ROW_BLOCKS = 4
POOL_TILE = 256
LANES = 128
SUBLANES = 8
V7X_VMEM_LIMIT = 56 * 1024 * 1024
MOD_ROWS = 8


def _cparams(*sem):
    return pltpu.CompilerParams(dimension_semantics=sem, vmem_limit_bytes=V7X_VMEM_LIMIT)


def _pow2_tile(limit, *sizes):
    t = limit
    while any(s % t for s in sizes):
        t //= 2
    return t


def _sigmoid(v):
    return 0.5 * jnp.tanh(0.5 * v) + 0.5


def _silu(v):
    return v * _sigmoid(v)


def _gelu_tanh(v):
    c = math.sqrt(2.0 / math.pi)
    return 0.5 * v * (1.0 + jnp.tanh(c * (v + 0.044715 * (v * v * v))))


def _rms(x, g):
    ms = jnp.mean(x * x, axis=-1, keepdims=True)
    return x * lax.rsqrt(ms + RMS_EPS) * g


def _bdot(a, b):
    return jnp.dot(a, b, preferred_element_type=F32)


def _mod_kernel(c_ref, w_ref, b_ref, o_ref):
    s = _silu(c_ref[...])
    o_ref[...] = jnp.dot(s, w_ref[...], preferred_element_type=F32, precision=HIGHEST) + b_ref[...]


def _adaln_tables(c_rows, w_mod, b_mod):
    depth, d, _ = w_mod.shape
    nrow = c_rows.shape[0]
    out = pl.pallas_call(
        _mod_kernel,
        grid=(depth, N_MOD),
        in_specs=[
            pl.BlockSpec((nrow, d), lambda l, j: (0, 0)),
            pl.BlockSpec((None, d, d), lambda l, j: (l, 0, j)),
            pl.BlockSpec((None, None, 1, d), lambda l, j: (l, j, 0, 0)),
        ],
        out_specs=pl.BlockSpec((None, None, nrow, d), lambda l, j: (l, j, 0, 0)),
        out_shape=jax.ShapeDtypeStruct((depth, N_MOD, nrow, d), F32),
        compiler_params=_cparams("parallel", "parallel"),
        name="adaln_tables",
    )(c_rows, w_mod, b_mod.reshape(depth, N_MOD, 1, d))
    out = jnp.transpose(out, (0, 2, 1, 3))
    return jnp.pad(out, ((0, 0), (0, 0), (0, MOD_ROWS - N_MOD), (0, 0)))


def _token_rows(chunk_ref, n_tok):
    n = chunk_ref.shape[0] // n_tok
    return jnp.concatenate([chunk_ref[pl.ds(c, n_tok, stride=n), :] for c in range(n)], axis=1)


def _inproj_kernel(*refs, after_moe, lat_tiles):
    if after_moe:
        (x1_ref, y_ref, modp_ref, mod_ref, g_ref, w_ref,
         x_out_ref, up_ref, us_ref, gp_ref, gs_ref, zs_ref) = refs
        x = x1_ref[...] + modp_ref[5:6, :] * _token_rows(y_ref, x1_ref.shape[0])
    else:
        (xl_ref, xc_ref, mod_ref, g_ref, w_ref,
         x_out_ref, up_ref, us_ref, gp_ref, gs_ref, zs_ref) = refs
        x = jnp.where(pl.program_id(0) < lat_tiles, xl_ref[...], xc_ref[...])
    x_out_ref[...] = x
    h = _rms(x, g_ref[...]) * (1.0 + mod_ref[1:2, :]) + mod_ref[0:1, :]
    hb = h.astype(BF16)
    d = gp_ref.shape[1]
    o0, o1, o2 = POOL_WIDTH, POOL_WIDTH + SSM_WIDTH, POOL_WIDTH + SSM_WIDTH + d
    up_ref[...] = _bdot(hb, w_ref[:, 0:o0])
    gp_ref[...] = _bdot(hb, w_ref[:, o1:o2]).astype(BF16)
    gs_ref[...] = _bdot(hb, w_ref[:, o2:]).astype(BF16)
    zs = _bdot(hb, w_ref[:, o0:o1])
    n_chunks = zs_ref.shape[1] // SSM_CHUNK
    for j in range(zs_ref.shape[0]):
        zs_ref[j] = zs[:, j * LANES:(j + 1) * LANES]
    for t in range(SSM_CHUNK):
        for j in range(zs_ref.shape[0]):
            piece = zs_ref[j, pl.ds(t, n_chunks, stride=SSM_CHUNK), :]
            us_ref[t, :, j * LANES:(j + 1) * LANES] = piece.astype(BF16)


def _mod_index(i, lat_tiles, tiles_per_batch, n_batch):
    return jnp.where(i < lat_tiles, i // tiles_per_batch, n_batch)


def _inproj(stream, mod_l, g1, w_in_bf, geo):
    after_moe = len(stream) == 3
    d = stream[0].shape[1]
    rows = geo["t_lat"] + geo["batch"] * geo["ctx"]
    tm = geo["tm"]
    lat_tiles = geo["t_lat"] // tm
    mod_map = lambda i: (_mod_index(i, lat_tiles, geo["seq"] // tm, geo["batch"]), 0, 0)
    row_map = lambda i: (i, 0)
    row_blk = pl.BlockSpec((tm, d), row_map)
    mod_blk = pl.BlockSpec((None, MOD_ROWS, d), mod_map)
    if after_moe:
        stream_specs = [row_blk, pl.BlockSpec((tm * (d // LANES), LANES), row_map), mod_blk]
    else:
        stream_specs = [pl.BlockSpec((tm, d), lambda i: (jnp.minimum(i, lat_tiles - 1), 0)),
                        pl.BlockSpec((tm, d), lambda i: (jnp.maximum(i - lat_tiles, 0), 0))]
    return pl.pallas_call(
        functools.partial(_inproj_kernel, after_moe=after_moe, lat_tiles=lat_tiles),
        grid=(rows // tm,),
        in_specs=stream_specs + [
            mod_blk,
            pl.BlockSpec((1, d), lambda i: (0, 0)),
            pl.BlockSpec(w_in_bf.shape, lambda i: (0, 0)),
        ],
        out_specs=[
            row_blk,
            pl.BlockSpec((tm, POOL_WIDTH), row_map),
            pl.BlockSpec((SSM_CHUNK, tm // SSM_CHUNK, SSM_WIDTH), lambda i: (0, i, 0)),
            row_blk,
            row_blk,
        ],
        out_shape=[
            jax.ShapeDtypeStruct((rows, d), F32),
            jax.ShapeDtypeStruct((rows, POOL_WIDTH), F32),
            jax.ShapeDtypeStruct((SSM_CHUNK, rows // SSM_CHUNK, SSM_WIDTH), BF16),
            jax.ShapeDtypeStruct((rows, d), BF16),
            jax.ShapeDtypeStruct((rows, d), BF16),
        ],
        scratch_shapes=[pltpu.VMEM((SSM_WIDTH // LANES, tm, LANES), F32)],
        compiler_params=_cparams("parallel"),
        name="inproj",
    )(*stream, mod_l, g1, w_in_bf)


def _box_count(pos, w, n):
    return np.minimum(pos + w // 2, n) - np.maximum(pos - w // 2, 0)


def _split_bf16(v):
    hi = v.astype(BF16)
    lo = (v - hi.astype(F32)).astype(BF16)
    return hi, lo


def _pool_kernel(u_ref, a_ref, inv_ref, pw_ref, ps_ref, *rest, n_tok, grid_rows):
    o_ref = rest[0]
    pad_ref = rest[1] if grid_rows is not None else None
    gi = pl.program_id(1)
    tile = min(POOL_TILE, n_tok)
    n_tiles = n_tok // tile
    halo = (max(POOL_WINDOWS) // 2) * GRID_W

    for widx, w in enumerate(POOL_WINDOWS):

        @pl.when(gi == widx)
        def _(w=w):
            a = a_ref[...]

            def colsum(k):
                v = u_ref[pl.ds(k * tile, tile), :]
                hi, lo = _split_bf16(v)
                return v, _bdot(a, hi) + _bdot(a, lo)

            def finish(k, v, total, inv):
                p = (total * inv - v).astype(BF16)
                y = _bdot(p, pw_ref[...]) * ps_ref[...]
                o_ref[pl.ds(k * tile, tile), :] = y.astype(o_ref.dtype)

            if grid_rows is None:
                for k in range(n_tiles):
                    v, total = colsum(k)
                    finish(k, v, total, inv_ref[...])
            else:
                rows_per_tile = tile // GRID_W
                zeros = jnp.zeros((halo, POOL_GROUP_DIM), F32)
                pad_ref[pl.ds(0, halo), :] = zeros
                pad_ref[pl.ds(halo + n_tok, halo), :] = zeros
                for k in range(n_tiles):
                    _, total = colsum(k)
                    pad_ref[pl.ds(halo + k * tile, tile), :] = total
                for k in range(n_tiles):
                    acc = None
                    for sh in range(-(w // 2), w // 2):
                        part = pad_ref[pl.ds(halo + k * tile + sh * GRID_W, tile), :]
                        acc = part if acc is None else acc + part
                    inv = jnp.concatenate([
                        inv_ref[pl.ds(q * GRID_W, GRID_W), :]
                        * float(1.0 / _box_count(k * rows_per_tile + q, w, grid_rows))
                        for q in range(rows_per_tile)], axis=0)
                    finish(k, u_ref[pl.ds(k * tile, tile), :], acc, inv)


def _window_matrix(w, n, period):
    t = np.arange(n)[:, None]
    s = np.arange(n)[None, :]
    inside = (s >= t - w // 2) & (s < t - w // 2 + w) & (t // period == s // period)
    return inside.astype(np.float32)


def _pool_branch(u_pool, pool_w_bf, pool_scale, need_ctx, geo):
    batch, seq, ctx_len, t_lat = geo["batch"], geo["seq"], geo["ctx"], geo["t_lat"]
    n_groups = len(POOL_WINDOWS)
    ps = pool_scale.reshape(n_groups, 1, POOL_GROUP_DIM)
    tile = min(POOL_TILE, seq)
    halo = (max(POOL_WINDOWS) // 2) * GRID_W

    def call(name, n_tok, row0, a_dim, period, grid_rows, scratch):
        a_mat = jnp.asarray(np.stack([_window_matrix(w, a_dim, period) for w in POOL_WINDOWS]), BF16)
        inv = np.stack([1.0 / _box_count(np.arange(a_dim) % period, w, period) for w in POOL_WINDOWS])
        inv = jnp.asarray(np.broadcast_to(inv[:, :, None], inv.shape + (POOL_GROUP_DIM,)), F32)
        return pl.pallas_call(
            functools.partial(_pool_kernel, n_tok=n_tok, grid_rows=grid_rows),
            grid=(batch, n_groups),
            in_specs=[
                pl.BlockSpec((n_tok, POOL_GROUP_DIM), lambda b, g: (row0 + b, g)),
                pl.BlockSpec((None, a_dim, a_dim), lambda b, g: (g, 0, 0)),
                pl.BlockSpec((None, a_dim, POOL_GROUP_DIM), lambda b, g: (g, 0, 0)),
                pl.BlockSpec((None, POOL_GROUP_DIM, POOL_GROUP_DIM), lambda b, g: (g, 0, 0)),
                pl.BlockSpec((None, 1, POOL_GROUP_DIM), lambda b, g: (g, 0, 0)),
            ],
            out_specs=pl.BlockSpec((n_tok, POOL_GROUP_DIM), lambda b, g: (b, g)),
            out_shape=jax.ShapeDtypeStruct((batch * n_tok, POOL_WIDTH), BF16),
            scratch_shapes=scratch,
            compiler_params=_cparams("parallel", "parallel"),
            name=name,
        )(u_pool, a_mat, inv, pool_w_bf, ps)

    p_lat = call("pool_lat", seq, 0, tile, GRID_W, seq // GRID_W,
                 [pltpu.VMEM((seq + 2 * halo, POOL_GROUP_DIM), F32)])
    if not need_ctx:
        return p_lat, None
    return p_lat, call("pool_ctx", ctx_len, t_lat // ctx_len, ctx_len, ctx_len, None, [])


def _cmul(ar, ai, br, bi):
    return ar * br - ai * bi, ar * bi + ai * br


def _zoh_abar(a_re, a_im, log_dt):
    dt = jnp.exp(log_dt)
    mag = jnp.exp(a_re * dt)
    return mag * jnp.cos(a_im * dt), mag * jnp.sin(a_im * dt)


def _ssm_op_kernel(ar_ref, al_ref, bt_ref, c2_ref, cr_ref, d_ref, rep_ref,
                   m_ref, wst_ref, wout_ref, apow_ref, ak_scr, bb_scr, k_scr):
    step = pl.program_id(2)
    lc, ns = SSM_CHUNK, SSM_STATE
    n_g8 = SSM_GROUPS // SSM_OCTETS

    row = lax.broadcasted_iota(jnp.int32, (LANES, n_g8 * ns), 0) // SSM_GROUP_DIM
    col = lax.broadcasted_iota(jnp.int32, (LANES, n_g8 * ns), 1) // ns
    group_mask_wide = row == col
    row = lax.broadcasted_iota(jnp.int32, (LANES, LANES), 0)
    col = lax.broadcasted_iota(jnp.int32, (LANES, LANES), 1)
    group_mask = (row // SSM_GROUP_DIM) == (col // SSM_GROUP_DIM)
    skip = jnp.where(row == col, jnp.broadcast_to(d_ref[...], (LANES, LANES)), 0.0)

    @pl.when(step == 0)
    def _():
        for d in range(2):
            a_re, a_im, log_dt = ar_ref[d, 0], ar_ref[d, 1], ar_ref[d, 2]
            abar_re, abar_im = _zoh_abar(a_re, a_im, log_dt)
            nr, ni = abar_re - 1.0, abar_im
            den = a_re * a_re + a_im * a_im
            f_re = (nr * a_re + ni * a_im) / den
            f_im = (ni * a_re - nr * a_im) / den
            bb_re, bb_im = _cmul(f_re, f_im, bt_ref[d, 0], bt_ref[d, 1])
            bb_scr[d, 0] = bb_re
            bb_scr[d, 1] = bb_im
            p_re, p_im = jnp.ones_like(abar_re), jnp.zeros_like(abar_im)
            ak_scr[d, 0, 0] = p_re
            ak_scr[d, 0, 1] = p_im
            for k in range(1, lc + 1):
                p_re, p_im = _cmul(p_re, p_im, abar_re, abar_im)
                ak_scr[d, k, 0] = p_re
                ak_scr[d, k, 1] = p_im
            for k in range(lc):
                ab_re, ab_im = _cmul(ak_scr[d, k, 0], ak_scr[d, k, 1], bb_re, bb_im)
                full = (jnp.dot(ab_re, cr_ref[d, 0], preferred_element_type=F32, precision=HIGHEST)
                        - jnp.dot(ab_im, cr_ref[d, 1], preferred_element_type=F32, precision=HIGHEST))
                k_scr[d, k] = jnp.where(group_mask, full, 0.0)

        apow_ref[...] = jnp.zeros_like(apow_ref)
        for d in range(2):
            p_re, p_im = _zoh_abar(al_ref[d, 0:1, :], al_ref[d, 1:2, :], al_ref[d, 2:3, :])
            for _ in range(int(math.log2(lc))):
                p_re, p_im = _cmul(p_re, p_im, p_re, p_im)
            for q in range(apow_ref.shape[0]):
                apow_ref[q, 2 * d:2 * d + 1, :] = p_re[:, q * LANES:(q + 1) * LANES]
                apow_ref[q, 2 * d + 1:2 * d + 2, :] = p_im[:, q * LANES:(q + 1) * LANES]

    def widen(x):
        full = _bdot(x.astype(BF16), rep_ref[...])
        return jnp.where(group_mask_wide, full, 0.0)

    part_w = n_g8 * ns
    for sub in range(OP_ROW_BLOCKS):
        s = step * OP_ROW_BLOCKS + sub
        blk_rows = slice(sub * LANES, (sub + 1) * LANES)
        fwd = _cmul(ak_scr[0, lc - 1 - s, 0], ak_scr[0, lc - 1 - s, 1], bb_scr[0, 0], bb_scr[0, 1])
        bwd = _cmul(ak_scr[1, s, 0], ak_scr[1, s, 1], bb_scr[1, 0], bb_scr[1, 1])
        for p, x in enumerate(fwd + bwd):
            wst_ref[blk_rows, p * part_w:(p + 1) * part_w] = widen(x).astype(BF16)

        cf_re, cf_im = _cmul(c2_ref[0, 0], c2_ref[0, 1], ak_scr[0, s + 1, 0], ak_scr[0, s + 1, 1])
        cb_re, cb_im = _cmul(c2_ref[1, 0], c2_ref[1, 1], ak_scr[1, lc - s, 0], ak_scr[1, lc - s, 1])
        for p, x in enumerate((cf_re, -cf_im, cb_re, -cb_im)):
            wout_ref[p * part_w:(p + 1) * part_w, blk_rows] = widen(x).T.astype(BF16)

        for t in range(lc):
            fwd_blk = k_scr[0, jnp.clip(t - s, 0, lc - 1)]
            bwd_blk = k_scr[1, jnp.clip(s - t, 0, lc - 1)]
            blk = jnp.where(t > s, fwd_blk, jnp.where(t < s, bwd_blk, fwd_blk + bwd_blk + skip))
            m_ref[blk_rows, t * LANES:(t + 1) * LANES] = blk.astype(BF16)


def _ssm_operators(a_re, a_im, log_dt, b_re, b_im, c_re, c_im, d_skip):
    depth = a_re.shape[0]
    ng, ns, nj, lc = SSM_GROUPS, SSM_STATE, SSM_GROUP_DIM, SSM_CHUNK
    n_oct, n_g8 = SSM_OCTETS, SSM_GROUPS // SSM_OCTETS
    width = lc * n_g8 * nj
    slabs = SSM_SLABS // n_oct

    def rows_layout(v):
        v = jnp.broadcast_to(v.reshape(depth, 2, n_oct, n_g8, 1, ns), (depth, 2, n_oct, n_g8, nj, ns))
        return v.reshape(depth, 2, n_oct, n_g8 * nj, ns)

    def lanes_layout(v):
        return v.reshape(depth, 2, n_oct, n_g8 * ns)

    dt_gp = jnp.broadcast_to(log_dt[..., None], a_re.shape)
    a_rows = jnp.stack([rows_layout(a_re), rows_layout(a_im), rows_layout(dt_gp)], axis=3)
    a_lanes = jnp.stack([lanes_layout(a_re), lanes_layout(a_im), lanes_layout(dt_gp)], axis=3)
    b_t = jnp.stack([jnp.swapaxes(b_re, -1, -2), jnp.swapaxes(b_im, -1, -2)], axis=3)
    b_t = b_t.reshape(depth, 2, n_oct, n_g8, 2, nj, ns)
    b_t = jnp.moveaxis(b_t, 4, 3).reshape(depth, 2, n_oct, 2, n_g8 * nj, ns)
    c_2 = jnp.stack([c_re, c_im], axis=2).reshape(depth, 2, 2, n_oct, n_g8 * nj, ns)
    c_2 = jnp.moveaxis(c_2, 2, 3)
    c_r = jnp.stack([c_re, c_im], axis=2).reshape(depth, 2, 2, n_oct, n_g8 * nj, ns)
    c_r = jnp.swapaxes(jnp.moveaxis(c_r, 2, 3), -1, -2)
    d_l = d_skip.reshape(depth, n_oct, 1, n_g8 * nj)
    rep = jnp.asarray(np.tile(np.eye(ns), (1, n_g8)), BF16)

    sel = lambda *blk: pl.BlockSpec(blk, lambda l, o, s: (l, 0, o) + (0,) * (len(blk) - 3))
    big = jax.ShapeDtypeStruct((depth, n_oct, width, width), BF16)
    return pl.pallas_call(
        _ssm_op_kernel,
        grid=(depth, n_oct, lc // OP_ROW_BLOCKS),
        in_specs=[
            sel(None, 2, None, 3, n_g8 * nj, ns),
            sel(None, 2, None, 3, n_g8 * ns),
            sel(None, 2, None, 2, n_g8 * nj, ns),
            sel(None, 2, None, 2, n_g8 * nj, ns),
            sel(None, 2, None, 2, ns, n_g8 * nj),
            pl.BlockSpec((None, None, 1, n_g8 * nj), lambda l, o, s: (l, o, 0, 0)),
            pl.BlockSpec(rep.shape, lambda l, o, s: (0, 0)),
        ],
        out_specs=[
            pl.BlockSpec((None, None, OP_ROW_BLOCKS * LANES, width), lambda l, o, s: (l, o, s, 0)),
            pl.BlockSpec((None, None, OP_ROW_BLOCKS * LANES, width), lambda l, o, s: (l, o, s, 0)),
            pl.BlockSpec((None, None, width, OP_ROW_BLOCKS * LANES), lambda l, o, s: (l, o, 0, s)),
            pl.BlockSpec((None, slabs, SUBLANES, LANES), lambda l, o, s: (l, o, 0, 0)),
        ],
        out_shape=[big, big, big,
                   jax.ShapeDtypeStruct((depth, SSM_SLABS, SUBLANES, LANES), F32)],
        scratch_shapes=[
            pltpu.VMEM((2, lc + 1, 2, n_g8 * nj, ns), F32),
            pltpu.VMEM((2, 2, n_g8 * nj, ns), F32),
            pltpu.VMEM((2, lc, LANES, LANES), F32),
        ],
        compiler_params=_cparams("arbitrary", "arbitrary", "arbitrary"),
        name="ssm_operators",
    )(a_rows, a_lanes, b_t, c_2, c_r, d_l, rep)


def _chunk_rows(u_ref):
    return jnp.concatenate([u_ref[t] for t in range(SSM_CHUNK)], axis=1)


def _ssm_state_kernel(u_ref, w_ref, *s_refs):
    s = _bdot(_chunk_rows(u_ref), w_ref[...])
    n_slab = s_refs[0].shape[0]
    for p, ref in enumerate(s_refs):
        for q in range(n_slab):
            col = (p * n_slab + q) * LANES
            ref[q] = s[:, col:col + LANES]


def _ssm_scan_kernel(sfr_ref, sfi_ref, sbr_ref, sbi_ref, a_ref, hfr_ref, hfi_ref, hbr_ref, hbi_ref,
                     *, n_lat_chunks, n_ctx_chunks, batch):
    shape = (batch, LANES)
    coef = [[jnp.broadcast_to(a_ref[k, j:j + 1, :], shape) for j in range(4)]
            for k in range(SCAN_SLABS)]
    zero = jnp.zeros(shape, F32)

    def chain(base, n, carry):
        def step(i, c):
            rf = pl.ds(base + i, batch, stride=n)
            rb = pl.ds(base + (n - 1) - i, batch, stride=n)
            out = []
            for k in range(SCAN_SLABS):
                fr, fi, br, bi = c[4 * k:4 * k + 4]
                afr, afi, abr, abi = coef[k]
                hfr_ref[k, rf, :] = fr
                hfi_ref[k, rf, :] = fi
                hbr_ref[k, rb, :] = br
                hbi_ref[k, rb, :] = bi
                out += [fr * afr - fi * afi + sfr_ref[k, rf, :], fi * afr + fr * afi + sfi_ref[k, rf, :],
                        br * abr - bi * abi + sbr_ref[k, rb, :], bi * abr + br * abi + sbi_ref[k, rb, :]]
            return tuple(out)

        return lax.fori_loop(0, n, step, carry, unroll=2)

    carry = chain(batch * n_lat_chunks, n_ctx_chunks, (zero,) * (4 * SCAN_SLABS))
    chain(0, n_lat_chunks, carry)


def _ssm_out_kernel(u_ref, hfr_ref, hfi_ref, hbr_ref, hbi_ref, m_ref, w_ref, y_ref):
    h = jnp.concatenate([ref[q] for ref in (hfr_ref, hfi_ref, hbr_ref, hbi_ref)
                         for q in range(ref.shape[0])], axis=1)
    y = _bdot(_chunk_rows(u_ref), m_ref[...]) + _bdot(h.astype(BF16), w_ref[...])
    for t in range(SSM_CHUNK):
        y_ref[t] = y[:, t * LANES:(t + 1) * LANES].astype(y_ref.dtype)


def _div_tile(n, limit, mult):
    return max(t for t in range(mult, min(n, limit) + 1, mult) if n % t == 0)


def _ssm_branch(u4, ops, layer, geo):
    m_mat, w_st, w_out, a_pow = ops
    batch, seq, ctx_len = geo["batch"], geo["seq"], geo["ctx"]
    lc = SSM_CHUNK
    n_rows = u4.shape[1]
    width = m_mat.shape[-1]
    n_parts = 4
    oct_slabs = SSM_SLABS // SSM_OCTETS
    bf16_rows = 2 * SUBLANES
    part_shape = jax.ShapeDtypeStruct((SSM_SLABS, n_rows, LANES), F32)

    def u_blk(rt):
        return pl.BlockSpec((lc, rt, LANES), lambda o, r: (0, r, o))

    def part_blk(rt):
        return pl.BlockSpec((oct_slabs, rt, LANES), lambda o, r: (o, r, 0))

    wmat_blk = pl.BlockSpec((None, None, width, width), lambda o, r: (layer, o, 0, 0))

    rt = _div_tile(n_rows, 544, bf16_rows)
    s_parts = pl.pallas_call(
        _ssm_state_kernel,
        grid=(SSM_OCTETS, n_rows // rt),
        in_specs=[u_blk(rt), wmat_blk],
        out_specs=[part_blk(rt)] * n_parts,
        out_shape=[part_shape] * n_parts,
        compiler_params=_cparams("parallel", "parallel"),
        name="ssm_state",
    )(u4, w_st)

    slab_blk = pl.BlockSpec((SCAN_SLABS, n_rows, LANES), lambda q: (q, 0, 0))
    h_parts = pl.pallas_call(
        functools.partial(_ssm_scan_kernel, n_lat_chunks=seq // lc, n_ctx_chunks=ctx_len // lc,
                          batch=batch),
        grid=(SSM_SLABS // SCAN_SLABS,),
        in_specs=[slab_blk] * n_parts + [pl.BlockSpec((None, SCAN_SLABS, SUBLANES, LANES),
                                                      lambda q: (layer, q, 0, 0))],
        out_specs=[slab_blk] * n_parts,
        out_shape=[part_shape] * n_parts,
        compiler_params=_cparams("parallel"),
        name="ssm_scan",
    )(*s_parts, a_pow)

    rt = _div_tile(n_rows, 272, bf16_rows)
    return pl.pallas_call(
        _ssm_out_kernel,
        grid=(SSM_OCTETS, n_rows // rt),
        in_specs=[u_blk(rt)] + [part_blk(rt)] * n_parts + [wmat_blk, wmat_blk],
        out_specs=u_blk(rt),
        out_shape=jax.ShapeDtypeStruct(u4.shape, BF16),
        compiler_params=_cparams("parallel", "parallel"),
        name="ssm_out",
    )(u4, *h_parts, m_mat, w_out)


def _route(scores_t, biased_t):
    sc = [scores_t[e:e + 1, :] for e in range(N_EXPERTS)]
    bc = [biased_t[e:e + 1, :] for e in range(N_EXPERTS)]
    npg = EXPERTS_PER_GROUP

    group_scores = []
    for g in range(N_EXPERT_GROUPS):
        v = bc[g * npg:(g + 1) * npg]
        best = None
        for i in range(npg):
            for j in range(i + 1, npg):
                pair = v[i] + v[j]
                best = pair if best is None else jnp.maximum(best, pair)
        group_scores.append(best)

    bg = jnp.zeros_like(group_scores[0], dtype=jnp.int32)
    best = group_scores[0]
    for g in range(1, N_EXPERT_GROUPS):
        upd = group_scores[g] > best
        bg = jnp.where(upd, g, bg)
        best = jnp.where(upd, group_scores[g], best)

    vb = list(bc[:npg])
    vs = list(sc[:npg])
    for g in range(1, N_EXPERT_GROUPS):
        sel = bg == g
        for j in range(npg):
            vb[j] = jnp.where(sel, bc[g * npg + j], vb[j])
            vs[j] = jnp.where(sel, sc[g * npg + j], vs[j])

    i1 = jnp.zeros_like(bg)
    m1, w1 = vb[0], vs[0]
    for j in range(1, npg):
        upd = vb[j] > m1
        i1 = jnp.where(upd, j, i1)
        m1 = jnp.where(upd, vb[j], m1)
        w1 = jnp.where(upd, vs[j], w1)

    neg = jnp.full_like(m1, -jnp.inf)
    i2 = jnp.zeros_like(bg)
    m2, w2 = neg, jnp.zeros_like(w1)
    for j in range(npg):
        cand = jnp.where(i1 == j, neg, vb[j])
        upd = cand > m2
        i2 = jnp.where(upd, j, i2)
        m2 = jnp.where(upd, cand, m2)
        w2 = jnp.where(upd, vs[j], w2)

    den = w1 + w2
    w1, w2 = w1 / den, w2 / den
    gates = [jnp.where(i1 == j, w1, 0.0) + jnp.where(i2 == j, w2, 0.0) for j in range(npg)]
    return bg, gates


def _merge_kernel(x_ref, y_ref, gp_ref, gs_ref, mod_ref, g2_ref, gluw_ref, glub_ref,
                  wbp_ref, wbs_ref, wo_ref, rw_ref, rb_ref, plat_ref, *rest, lat_tiles):
    if len(rest) == 5:
        pc_ref, x1_ref, h2_ref, grp_ref, ys_ref = rest
        pooled = jnp.where(pl.program_id(0) < lat_tiles, plat_ref[...], pc_ref[...])
    else:
        x1_ref, h2_ref, grp_ref, ys_ref = rest
        pooled = plat_ref[...]
    n_chunks = ys_ref.shape[1] // SSM_CHUNK
    n_col = ys_ref.shape[0]
    for t in range(SSM_CHUNK):
        yt = y_ref[t].astype(F32)
        for j in range(n_col):
            ys_ref[j, pl.ds(t, n_chunks, stride=SSM_CHUNK), :] = yt[:, j * LANES:(j + 1) * LANES]
    s = _gelu_tanh(jnp.concatenate([ys_ref[j] for j in range(n_col)], axis=1))
    s = s * _sigmoid(_bdot(s.astype(BF16), gluw_ref[...]) + glub_ref[...])
    m = (_sigmoid(gp_ref[...].astype(F32)) * _bdot(pooled, wbp_ref[...])
         + _sigmoid(gs_ref[...].astype(F32)) * _bdot(s.astype(BF16), wbs_ref[...]))
    x1 = x_ref[...] + mod_ref[2:3, :] * _bdot(m.astype(BF16), wo_ref[...])
    x1_ref[...] = x1
    h2 = _rms(x1, g2_ref[...]) * (1.0 + mod_ref[4:5, :]) + mod_ref[3:4, :]
    tm, d = h2.shape
    n_chunk = d // LANES + 1
    for c in range(n_chunk - 1):
        h2_ref[pl.ds(c, tm, stride=n_chunk), :] = h2[:, c * LANES:(c + 1) * LANES]

    hi, lo = _split_bf16(h2)
    r1 = _bdot(hi, rw_ref[...])
    logits = r1[:, 0:LANES] + (r1[:, LANES:] + _bdot(lo, rw_ref[:, 0:LANES]))
    scores = _sigmoid(logits)
    bg, gates = _route(scores.T, (scores + rb_ref[...]).T)
    row = lax.broadcasted_iota(jnp.int32, (SUBLANES, tm), 0)
    grp_ref[...] = jnp.broadcast_to(bg, (SUBLANES, tm))
    g8 = jnp.zeros((SUBLANES, tm), F32)
    for j, g in enumerate(gates):
        g8 = jnp.where(row == j, g, g8)
    g_t = jnp.concatenate([g8, jnp.zeros((LANES - SUBLANES, tm), F32)], axis=0)
    h2_ref[pl.ds(n_chunk - 1, tm, stride=n_chunk), :] = g_t.T


def _merge(x_all, y4, pooled, g_pool, g_ssm, mod_l, g2, wts, router, n_rows, geo):
    d = x_all.shape[1]
    tm = geo["tm"]
    n_chunk = d // LANES + 1
    lat_tiles = geo["t_lat"] // tm
    mod_map = lambda i: (_mod_index(i, lat_tiles, geo["seq"] // tm, geo["batch"]), 0, 0)
    row_map = lambda i: (i, 0)
    full = lambda a: pl.BlockSpec(a.shape, lambda i: (0,) * a.ndim)
    glu_w, glu_b, w_bp, w_bs, w_o = wts
    rw, rb = router
    p_lat, p_ctx = pooled
    pooled_specs = [pl.BlockSpec((tm, POOL_WIDTH), lambda i: (jnp.minimum(i, lat_tiles - 1), 0))]
    pooled_args = [p_lat]
    if p_ctx is not None:
        pooled_specs.append(pl.BlockSpec((tm, POOL_WIDTH), lambda i: (jnp.maximum(i - lat_tiles, 0), 0)))
        pooled_args.append(p_ctx)
    return pl.pallas_call(
        functools.partial(_merge_kernel, lat_tiles=lat_tiles),
        grid=(n_rows // tm,),
        in_specs=[
            pl.BlockSpec((tm, d), row_map),
            pl.BlockSpec((SSM_CHUNK, tm // SSM_CHUNK, SSM_WIDTH), lambda i: (0, i, 0)),
            pl.BlockSpec((tm, d), row_map),
            pl.BlockSpec((tm, d), row_map),
            pl.BlockSpec((None, MOD_ROWS, d), mod_map),
            full(g2), full(glu_w), full(glu_b), full(w_bp), full(w_bs), full(w_o), full(rw), full(rb),
        ] + pooled_specs,
        out_specs=[
            pl.BlockSpec((tm, d), row_map),
            pl.BlockSpec((tm * n_chunk, LANES), row_map),
            pl.BlockSpec((SUBLANES, tm), lambda i: (0, i)),
        ],
        out_shape=[
            jax.ShapeDtypeStruct((n_rows, d), F32),
            jax.ShapeDtypeStruct((n_rows * n_chunk, LANES), F32),
            jax.ShapeDtypeStruct((SUBLANES, n_rows), jnp.int32),
        ],
        scratch_shapes=[pltpu.VMEM((SSM_WIDTH // LANES, tm, LANES), F32)],
        compiler_params=_cparams("parallel"),
        name="merge_router",
    )(x_all, y4, g_pool, g_ssm, mod_l, g2, glu_w, glu_b, w_bp, w_bs, w_o, rw, rb, *pooled_args)


def _moe_plan(group, tile):
    n_rows = group.shape[0]
    n_tiles = n_rows // tile + N_EXPERT_GROUPS
    onehot = (group[:, None] == jnp.arange(N_EXPERT_GROUPS)[None, :]).astype(jnp.int32)
    counts = jnp.sum(onehot, axis=0)
    padded = ((counts + tile - 1) // tile) * tile
    ends = jnp.cumsum(padded)
    starts = ends - padded
    rank = jnp.sum((jnp.cumsum(onehot, axis=0) - 1) * onehot, axis=1)
    pos = jnp.sum(onehot * starts[None, :], axis=1) + rank
    token = jnp.arange(n_rows, dtype=jnp.int32)
    slot = jnp.arange(n_tiles * tile, dtype=jnp.int32)
    src = jnp.zeros((n_tiles * tile,), jnp.int32).at[pos].set(
        token, unique_indices=True, mode="promise_in_bounds")
    slot_group = jnp.minimum(jnp.sum(slot[:, None] >= ends[None, :], axis=1), N_EXPERT_GROUPS - 1)
    used = slot < (starts + counts)[slot_group]
    dst = jnp.where(used, src, n_rows + slot % tile)
    tile_start = jnp.arange(n_tiles, dtype=jnp.int32) * tile
    tile_group = jnp.minimum(jnp.sum(tile_start[:, None] >= ends[None, :], axis=1),
                             N_EXPERT_GROUPS - 1).astype(jnp.int32)
    n_used = (ends[-1] // tile).astype(jnp.int32).reshape(1)
    shape3 = (n_tiles, 1, tile)
    return tile_group, n_used, src.reshape(shape3), dst.reshape(shape3)


def _moe_kernel(tg_ref, nu_ref, src_ref, nxt_ref, dst_ref, h_hbm, w1_ref, w3_ref, w2_ref, y_hbm,
                hbuf, ybuf, gsem, ssem, *, tile, n_in, n_out):
    i = pl.program_id(0)
    n_used = nu_ref[0]
    slot = i % 2

    def row_copy_in(idx_ref, r, s):
        return pltpu.make_async_copy(h_hbm.at[pl.ds(idx_ref[0, r], n_in)],
                                     hbuf.at[s, pl.ds(r * n_in, n_in)], gsem.at[s])

    def row_copy_out(r, s):
        first = pl.multiple_of(dst_ref[0, r], n_out)
        return pltpu.make_async_copy(ybuf.at[s, pl.ds(pl.multiple_of(r * n_out, n_out), n_out)],
                                     y_hbm.at[pl.ds(first, n_out)], ssem.at[s])

    def tile_copy_in(s):
        return pltpu.make_async_copy(h_hbm.at[pl.ds(0, tile * n_in)], hbuf.at[s], gsem.at[s])

    def tile_copy_out(s):
        return pltpu.make_async_copy(ybuf.at[s], y_hbm.at[pl.ds(0, tile * n_out)], ssem.at[s])

    def start_rows(make):
        def body(r, carry):
            make(r).start()
            return carry
        lax.fori_loop(0, tile, body, 0, unroll=16)

    @pl.when(i == 0)
    def _():
        start_rows(lambda r: row_copy_in(src_ref, r, 0))
        ybuf[1] = jnp.zeros(ybuf.shape[1:], F32)
        spare = pltpu.make_async_copy(
            ybuf.at[1], y_hbm.at[pl.ds(y_hbm.shape[0] - tile * n_out, tile * n_out)], ssem.at[1])
        spare.start()
        spare.wait()

    @pl.when(i + 1 < n_used)
    def _():
        start_rows(lambda r: row_copy_in(nxt_ref, r, 1 - slot))

    @pl.when(i < n_used)
    def _():
        tile_copy_in(slot).wait()
        hrows, yrows = hbuf.at[slot], ybuf.at[slot]
        h = jnp.concatenate([hrows[pl.ds(c, tile, stride=n_in), :] for c in range(n_out)],
                            axis=1).astype(BF16)
        gates = hrows[pl.ds(n_in - 1, tile, stride=n_in), :]
        acc = None
        for e in range(EXPERTS_PER_GROUP):
            hid = (_silu(_bdot(h, w1_ref[e])) * _bdot(h, w3_ref[e])).astype(BF16)
            y = gates[:, e:e + 1] * _bdot(hid, w2_ref[e])
            acc = y if acc is None else acc + y

        @pl.when(i >= 2)
        def _():
            tile_copy_out(slot).wait()

        for c in range(n_out):
            yrows[pl.ds(c, tile, stride=n_out), :] = acc[:, c * LANES:(c + 1) * LANES]
        start_rows(lambda r: row_copy_out(r, slot))

    @pl.when(i == pl.num_programs(0) - 1)
    def _():
        tile_copy_out((n_used - 1) % 2).wait()

        @pl.when(n_used >= 2)
        def _():
            tile_copy_out(n_used % 2).wait()


def _moe(h2g, group, w1, w3, w2, geo):
    npg = EXPERTS_PER_GROUP
    d, d_exp = w1.shape[-2:]
    n_out = d // LANES
    n_in = n_out + 1
    n_rows = h2g.shape[0] // n_in
    tile = geo["tm_moe"]
    tile_group, n_used, src, dst = _moe_plan(group, tile)
    src, dst = src * n_in, dst * n_out
    n_tiles = tile_group.shape[0]
    idx_blk = lambda f: pl.BlockSpec((None, 1, tile), f, memory_space=pltpu.SMEM)
    w_map = lambda i, tg, nu: (tg[i], 0, 0, 0)
    grid_spec = pltpu.PrefetchScalarGridSpec(
        num_scalar_prefetch=2,
        grid=(n_tiles,),
        in_specs=[
            idx_blk(lambda i, tg, nu: (i, 0, 0)),
            idx_blk(lambda i, tg, nu: (jnp.minimum(i + 1, n_tiles - 1), 0, 0)),
            idx_blk(lambda i, tg, nu: (i, 0, 0)),
            pl.BlockSpec(memory_space=pl.ANY),
            pl.BlockSpec((None, npg, d, d_exp), w_map),
            pl.BlockSpec((None, npg, d, d_exp), w_map),
            pl.BlockSpec((None, npg, d_exp, d), w_map),
        ],
        out_specs=pl.BlockSpec(memory_space=pl.ANY),
        scratch_shapes=[
            pltpu.VMEM((2, tile * n_in, LANES), F32),
            pltpu.VMEM((2, tile * n_out, LANES), F32),
            pltpu.SemaphoreType.DMA((2,)),
            pltpu.SemaphoreType.DMA((2,)),
        ],
    )
    return pl.pallas_call(
        functools.partial(_moe_kernel, tile=tile, n_in=n_in, n_out=n_out),
        grid_spec=grid_spec,
        out_shape=jax.ShapeDtypeStruct(((n_rows + tile) * n_out, LANES), F32),
        compiler_params=_cparams("arbitrary"),
        name="moe_grouped",
    )(tile_group, n_used, src, src, dst, h2g, w1, w3, w2)


def _residual_kernel(x1_ref, y_ref, mod_ref, fg_ref, o_ref):
    x2 = x1_ref[...] + mod_ref[5:6, :] * _token_rows(y_ref, x1_ref.shape[0])
    o_ref[...] = _rms(x2, fg_ref[...])


def _moe_residual(x1, y_moe, mod_l, final_g, geo):
    n_rows, d = x1.shape
    tm = geo["tm"]
    n_out = d // LANES
    mod_map = lambda i: (_mod_index(i, geo["t_lat"] // tm, geo["seq"] // tm, geo["batch"]), 0, 0)
    row_map = lambda i: (i, 0)
    return pl.pallas_call(
        _residual_kernel,
        grid=(n_rows // tm,),
        in_specs=[
            pl.BlockSpec((tm, d), row_map),
            pl.BlockSpec((tm * n_out, LANES), row_map),
            pl.BlockSpec((None, MOD_ROWS, d), mod_map),
            pl.BlockSpec((1, d), lambda i: (0, 0)),
        ],
        out_specs=pl.BlockSpec((tm, d), row_map),
        out_shape=jax.ShapeDtypeStruct((n_rows, d), F32),
        compiler_params=_cparams("parallel"),
        name="moe_residual",
    )(x1, y_moe, mod_l, final_g)


def kernel(x, c, ctx, c_ctx, w_mod, b_mod, norm1_g, norm2_g, w_in, pool_w, pool_scale, ssm_a_re, ssm_a_im, ssm_log_dt, ssm_b_re, ssm_b_im, ssm_c_re, ssm_c_im, ssm_d, glu_w, glu_b, w_branch_pool, w_branch_ssm, w_out, router_w, router_b, expert_w1, expert_w3, expert_w2, final_g):
    batch, seq, d = x.shape
    ctx_len = ctx.shape[1]
    depth = w_mod.shape[0]
    t_lat, t_ctx = batch * seq, batch * ctx_len
    assert seq % GRID_W == 0 and seq % POOL_TILE == 0 and ctx_len % SSM_CHUNK == 0
    assert t_lat % ctx_len == 0 and batch == SUBLANES and ctx_len <= POOL_TILE
    geo = dict(batch=batch, seq=seq, ctx=ctx_len, t_lat=t_lat,
               tm=_pow2_tile(512, seq, t_ctx), tm_moe=_pow2_tile(512, seq, t_ctx))

    n_cond = 2 * SUBLANES
    c_rows = jnp.concatenate([c, c_ctx[None, :], jnp.zeros((n_cond - batch - 1, d), F32)], axis=0)
    mod = _adaln_tables(c_rows, w_mod, b_mod)

    rw = jnp.pad(router_w, ((0, 0), (0, LANES - N_EXPERTS)))
    rw_hi, rw_lo = _split_bf16(rw)
    rw_cat = jnp.concatenate([rw_hi, rw_lo], axis=1)
    rb = jnp.pad(router_b, (0, LANES - N_EXPERTS)).reshape(1, LANES)
    fg = final_g.reshape(1, d)
    d_exp = expert_w1.shape[-1]
    grouped = lambda w, a, b: w.astype(BF16).reshape(N_EXPERT_GROUPS, EXPERTS_PER_GROUP, a, b)

    ssm_ops = _ssm_operators(ssm_a_re, ssm_a_im, ssm_log_dt, ssm_b_re, ssm_b_im, ssm_c_re, ssm_c_im,
                             ssm_d)
    stream = (x.reshape(t_lat, d), ctx.reshape(t_ctx, d))
    for i in range(depth):
        last = i == depth - 1
        n_rows = t_lat if last else t_lat + t_ctx
        x_all, u_pool, u4, g_pool, g_ssm = _inproj(
            stream, mod[i], norm1_g[i].reshape(1, d), w_in[i].astype(BF16), geo)
        pooled = _pool_branch(u_pool, pool_w[i].astype(BF16), pool_scale[i], not last, geo)
        y4 = _ssm_branch(u4, ssm_ops, i, geo)
        wts = (glu_w[i].astype(BF16), glu_b[i].reshape(1, SSM_WIDTH), w_branch_pool[i].astype(BF16),
               w_branch_ssm[i].astype(BF16), w_out[i].astype(BF16))
        x1, h2g, grp = _merge(x_all, y4, pooled, g_pool, g_ssm, mod[i],
                              norm2_g[i].reshape(1, d), wts, (rw_cat, rb), n_rows, geo)
        y_moe = _moe(h2g, grp[0], grouped(expert_w1[i], d, d_exp), grouped(expert_w3[i], d, d_exp),
                     grouped(expert_w2[i], d_exp, d), geo)
        stream = (x1, y_moe, mod[i])
    return _moe_residual(x1, y_moe, mod[depth - 1], fg, geo).reshape(batch, seq, d)
```

```python
import functools
import math

import numpy as np
import jax
import jax.numpy as jnp
from jax import lax
from jax.experimental import pallas as pl
from jax.experimental.pallas import tpu as pltpu

F32 = jnp.float32
BF16 = jnp.bfloat16
HIGHEST = lax.Precision.HIGHEST

GRID_W = 64
POOL_WINDOWS = (2, 4, 8, 16)
POOL_GROUP_DIM = 128
POOL_WIDTH = 512
SSM_WIDTH = 512
SSM_GROUP_DIM = 16
SSM_GROUPS = 32
SSM_STATE = 64
N_MOD = 6
N_EXPERTS = 16
EXPERTS_PER_GROUP = 4
N_EXPERT_GROUPS = 4
RMS_EPS = 1e-6

SSM_CHUNK = 16
SSM_OCTETS = SSM_GROUPS // 8
SSM_SLABS = SSM_GROUPS // 2
SCAN_SLABS = 1
OP_ROW_BLOCKS = 4
POOL_TILE = 256
LANES = 128
SUBLANES = 8
V7X_VMEM_LIMIT = 56 * 1024 * 1024
MOD_ROWS = 8


def _cparams(*sem):
    return pltpu.CompilerParams(dimension_semantics=sem, vmem_limit_bytes=V7X_VMEM_LIMIT)


def _pow2_tile(limit, *sizes):
    t = limit
    while any(s % t for s in sizes):
        t //= 2
    return t


def _sigmoid(v):
    return 0.5 * jnp.tanh(0.5 * v) + 0.5


def _silu(v):
    return v * _sigmoid(v)


def _gelu_tanh(v):
    c = math.sqrt(2.0 / math.pi)
    return 0.5 * v * (1.0 + jnp.tanh(c * (v + 0.044715 * (v * v * v))))


def _rms(x, g):
    ms = jnp.mean(x * x, axis=-1, keepdims=True)
    return x * lax.rsqrt(ms + RMS_EPS) * g


def _bdot(a, b):
    return jnp.dot(a, b, preferred_element_type=F32)


def _mod_kernel(c_ref, w_ref, b_ref, o_ref):
    s = _silu(c_ref[...])
    o_ref[...] = jnp.dot(s, w_ref[...], preferred_element_type=F32, precision=HIGHEST) + b_ref[...]


def _adaln_tables(c_rows, w_mod, b_mod):
    depth, d, _ = w_mod.shape
    nrow = c_rows.shape[0]
    out = pl.pallas_call(
        _mod_kernel,
        grid=(depth, N_MOD),
        in_specs=[
            pl.BlockSpec((nrow, d), lambda l, j: (0, 0)),
            pl.BlockSpec((None, d, d), lambda l, j: (l, 0, j)),
            pl.BlockSpec((None, None, 1, d), lambda l, j: (l, j, 0, 0)),
        ],
        out_specs=pl.BlockSpec((None, None, nrow, d), lambda l, j: (l, j, 0, 0)),
        out_shape=jax.ShapeDtypeStruct((depth, N_MOD, nrow, d), F32),
        compiler_params=_cparams("parallel", "parallel"),
        name="adaln_tables",
    )(c_rows, w_mod, b_mod.reshape(depth, N_MOD, 1, d))
    out = jnp.transpose(out, (0, 2, 1, 3))
    return jnp.pad(out, ((0, 0), (0, 0), (0, MOD_ROWS - N_MOD), (0, 0)))


def _token_rows(chunk_ref, n_tok):
    n = chunk_ref.shape[0] // n_tok
    return jnp.concatenate([chunk_ref[pl.ds(c, n_tok, stride=n), :] for c in range(n)], axis=1)


def _inproj_kernel(*refs, after_moe, lat_tiles):
    if after_moe:
        (x1_ref, y_ref, modp_ref, mod_ref, g_ref, w_ref,
         x_out_ref, up_ref, us_ref, gp_ref, gs_ref, zs_ref) = refs
        x = x1_ref[...] + modp_ref[5:6, :] * _token_rows(y_ref, x1_ref.shape[0])
    else:
        (xl_ref, xc_ref, mod_ref, g_ref, w_ref,
         x_out_ref, up_ref, us_ref, gp_ref, gs_ref, zs_ref) = refs
        x = jnp.where(pl.program_id(0) < lat_tiles, xl_ref[...], xc_ref[...])
    x_out_ref[...] = x
    h = _rms(x, g_ref[...]) * (1.0 + mod_ref[1:2, :]) + mod_ref[0:1, :]
    hb = h.astype(BF16)
    d = gp_ref.shape[1]
    o0, o1, o2 = POOL_WIDTH, POOL_WIDTH + SSM_WIDTH, POOL_WIDTH + SSM_WIDTH + d
    up_ref[...] = _bdot(hb, w_ref[:, 0:o0])
    gp_ref[...] = _bdot(hb, w_ref[:, o1:o2]).astype(BF16)
    gs_ref[...] = _bdot(hb, w_ref[:, o2:]).astype(BF16)
    zs = _bdot(hb, w_ref[:, o0:o1])
    n_chunks = zs_ref.shape[1] // SSM_CHUNK
    for j in range(zs_ref.shape[0]):
        zs_ref[j] = zs[:, j * LANES:(j + 1) * LANES]
    for t in range(SSM_CHUNK):
        for j in range(zs_ref.shape[0]):
            piece = zs_ref[j, pl.ds(t, n_chunks, stride=SSM_CHUNK), :]
            us_ref[t, :, j * LANES:(j + 1) * LANES] = piece.astype(BF16)


def _mod_index(i, lat_tiles, tiles_per_batch, n_batch):
    return jnp.where(i < lat_tiles, i // tiles_per_batch, n_batch)


def _inproj(stream, mod_l, g1, w_in_bf, geo):
    after_moe = len(stream) == 3
    d = stream[0].shape[1]
    rows = geo["t_lat"] + geo["batch"] * geo["ctx"]
    tm = geo["tm"]
    lat_tiles = geo["t_lat"] // tm
    mod_map = lambda i: (_mod_index(i, lat_tiles, geo["seq"] // tm, geo["batch"]), 0, 0)
    row_map = lambda i: (i, 0)
    row_blk = pl.BlockSpec((tm, d), row_map)
    mod_blk = pl.BlockSpec((None, MOD_ROWS, d), mod_map)
    if after_moe:
        stream_specs = [row_blk, pl.BlockSpec((tm * (d // LANES), LANES), row_map), mod_blk]
    else:
        stream_specs = [pl.BlockSpec((tm, d), lambda i: (jnp.minimum(i, lat_tiles - 1), 0)),
                        pl.BlockSpec((tm, d), lambda i: (jnp.maximum(i - lat_tiles, 0), 0))]
    return pl.pallas_call(
        functools.partial(_inproj_kernel, after_moe=after_moe, lat_tiles=lat_tiles),
        grid=(rows // tm,),
        in_specs=stream_specs + [
            mod_blk,
            pl.BlockSpec((1, d), lambda i: (0, 0)),
            pl.BlockSpec(w_in_bf.shape, lambda i: (0, 0)),
        ],
        out_specs=[
            row_blk,
            pl.BlockSpec((tm, POOL_WIDTH), row_map),
            pl.BlockSpec((SSM_CHUNK, tm // SSM_CHUNK, SSM_WIDTH), lambda i: (0, i, 0)),
            row_blk,
            row_blk,
        ],
        out_shape=[
            jax.ShapeDtypeStruct((rows, d), F32),
            jax.ShapeDtypeStruct((rows, POOL_WIDTH), F32),
            jax.ShapeDtypeStruct((SSM_CHUNK, rows // SSM_CHUNK, SSM_WIDTH), BF16),
            jax.ShapeDtypeStruct((rows, d), BF16),
            jax.ShapeDtypeStruct((rows, d), BF16),
        ],
        scratch_shapes=[pltpu.VMEM((SSM_WIDTH // LANES, tm, LANES), F32)],
        compiler_params=_cparams("parallel"),
        name="inproj",
    )(*stream, mod_l, g1, w_in_bf)


def _box_count(pos, w, n):
    return np.minimum(pos + w // 2, n) - np.maximum(pos - w // 2, 0)


def _split_bf16(v):
    hi = v.astype(BF16)
    lo = (v - hi.astype(F32)).astype(BF16)
    return hi, lo


def _pool_kernel(u_ref, a_ref, inv_ref, pw_ref, ps_ref, *rest, n_tok, grid_rows):
    o_ref = rest[0]
    pad_ref = rest[1] if grid_rows is not None else None
    gi = pl.program_id(1)
    tile = min(POOL_TILE, n_tok)
    n_tiles = n_tok // tile
    halo = (max(POOL_WINDOWS) // 2) * GRID_W

    for widx, w in enumerate(POOL_WINDOWS):

        @pl.when(gi == widx)
        def _(w=w):
            a = a_ref[...]

            def colsum(k):
                v = u_ref[pl.ds(k * tile, tile), :]
                hi, lo = _split_bf16(v)
                return v, _bdot(a, hi) + _bdot(a, lo)

            def finish(k, v, total, inv):
                p = (total * inv - v).astype(BF16)
                y = _bdot(p, pw_ref[...]) * ps_ref[...]
                o_ref[pl.ds(k * tile, tile), :] = y.astype(o_ref.dtype)

            if grid_rows is None:
                for k in range(n_tiles):
                    v, total = colsum(k)
                    finish(k, v, total, inv_ref[...])
            else:
                rows_per_tile = tile // GRID_W
                zeros = jnp.zeros((halo, POOL_GROUP_DIM), F32)
                pad_ref[pl.ds(0, halo), :] = zeros
                pad_ref[pl.ds(halo + n_tok, halo), :] = zeros
                for k in range(n_tiles):
                    _, total = colsum(k)
                    pad_ref[pl.ds(halo + k * tile, tile), :] = total
                for k in range(n_tiles):
                    acc = None
                    for sh in range(-(w // 2), w // 2):
                        part = pad_ref[pl.ds(halo + k * tile + sh * GRID_W, tile), :]
                        acc = part if acc is None else acc + part
                    inv = jnp.concatenate([
                        inv_ref[pl.ds(q * GRID_W, GRID_W), :]
                        * float(1.0 / _box_count(k * rows_per_tile + q, w, grid_rows))
                        for q in range(rows_per_tile)], axis=0)
                    finish(k, u_ref[pl.ds(k * tile, tile), :], acc, inv)


def _window_matrix(w, n, period):
    t = np.arange(n)[:, None]
    s = np.arange(n)[None, :]
    inside = (s >= t - w // 2) & (s < t - w // 2 + w) & (t // period == s // period)
    return inside.astype(np.float32)


def _pool_branch(u_pool, pool_w_bf, pool_scale, need_ctx, geo):
    batch, seq, ctx_len, t_lat = geo["batch"], geo["seq"], geo["ctx"], geo["t_lat"]
    n_groups = len(POOL_WINDOWS)
    ps = pool_scale.reshape(n_groups, 1, POOL_GROUP_DIM)
    tile = min(POOL_TILE, seq)
    halo = (max(POOL_WINDOWS) // 2) * GRID_W

    def call(name, n_tok, row0, a_dim, period, grid_rows, scratch):
        a_mat = jnp.asarray(np.stack([_window_matrix(w, a_dim, period) for w in POOL_WINDOWS]), BF16)
        inv = np.stack([1.0 / _box_count(np.arange(a_dim) % period, w, period) for w in POOL_WINDOWS])
        inv = jnp.asarray(np.broadcast_to(inv[:, :, None], inv.shape + (POOL_GROUP_DIM,)), F32)
        return pl.pallas_call(
            functools.partial(_pool_kernel, n_tok=n_tok, grid_rows=grid_rows),
            grid=(batch, n_groups),
            in_specs=[
                pl.BlockSpec((n_tok, POOL_GROUP_DIM), lambda b, g: (row0 + b, g)),
                pl.BlockSpec((None, a_dim, a_dim), lambda b, g: (g, 0, 0)),
                pl.BlockSpec((None, a_dim, POOL_GROUP_DIM), lambda b, g: (g, 0, 0)),
                pl.BlockSpec((None, POOL_GROUP_DIM, POOL_GROUP_DIM), lambda b, g: (g, 0, 0)),
                pl.BlockSpec((None, 1, POOL_GROUP_DIM), lambda b, g: (g, 0, 0)),
            ],
            out_specs=pl.BlockSpec((n_tok, POOL_GROUP_DIM), lambda b, g: (b, g)),
            out_shape=jax.ShapeDtypeStruct((batch * n_tok, POOL_WIDTH), BF16),
            scratch_shapes=scratch,
            compiler_params=_cparams("parallel", "parallel"),
            name=name,
        )(u_pool, a_mat, inv, pool_w_bf, ps)

    p_lat = call("pool_lat", seq, 0, tile, GRID_W, seq // GRID_W,
                 [pltpu.VMEM((seq + 2 * halo, POOL_GROUP_DIM), F32)])
    if not need_ctx:
        return p_lat, None
    return p_lat, call("pool_ctx", ctx_len, t_lat // ctx_len, ctx_len, ctx_len, None, [])


def _cmul(ar, ai, br, bi):
    return ar * br - ai * bi, ar * bi + ai * br


def _zoh_abar(a_re, a_im, log_dt):
    dt = jnp.exp(log_dt)
    mag = jnp.exp(a_re * dt)
    return mag * jnp.cos(a_im * dt), mag * jnp.sin(a_im * dt)


def _ssm_op_kernel(ar_ref, al_ref, bt_ref, c2_ref, cr_ref, d_ref, rep_ref,
                   m_ref, wst_ref, wout_ref, apow_ref, ak_scr, bb_scr, k_scr):
    step = pl.program_id(2)
    lc, ns = SSM_CHUNK, SSM_STATE
    n_g8 = SSM_GROUPS // SSM_OCTETS

    row = lax.broadcasted_iota(jnp.int32, (LANES, n_g8 * ns), 0) // SSM_GROUP_DIM
    col = lax.broadcasted_iota(jnp.int32, (LANES, n_g8 * ns), 1) // ns
    group_mask_wide = row == col
    row = lax.broadcasted_iota(jnp.int32, (LANES, LANES), 0)
    col = lax.broadcasted_iota(jnp.int32, (LANES, LANES), 1)
    group_mask = (row // SSM_GROUP_DIM) == (col // SSM_GROUP_DIM)
    skip = jnp.where(row == col, jnp.broadcast_to(d_ref[...], (LANES, LANES)), 0.0)

    @pl.when(step == 0)
    def _():
        for d in range(2):
            a_re, a_im, log_dt = ar_ref[d, 0], ar_ref[d, 1], ar_ref[d, 2]
            abar_re, abar_im = _zoh_abar(a_re, a_im, log_dt)
            nr, ni = abar_re - 1.0, abar_im
            den = a_re * a_re + a_im * a_im
            f_re = (nr * a_re + ni * a_im) / den
            f_im = (ni * a_re - nr * a_im) / den
            bb_re, bb_im = _cmul(f_re, f_im, bt_ref[d, 0], bt_ref[d, 1])
            bb_scr[d, 0] = bb_re
            bb_scr[d, 1] = bb_im
            p_re, p_im = jnp.ones_like(abar_re), jnp.zeros_like(abar_im)
            ak_scr[d, 0, 0] = p_re
            ak_scr[d, 0, 1] = p_im
            for k in range(1, lc + 1):
                p_re, p_im = _cmul(p_re, p_im, abar_re, abar_im)
                ak_scr[d, k, 0] = p_re
                ak_scr[d, k, 1] = p_im
            for k in range(lc):
                ab_re, ab_im = _cmul(ak_scr[d, k, 0], ak_scr[d, k, 1], bb_re, bb_im)
                full = (jnp.dot(ab_re, cr_ref[d, 0], preferred_element_type=F32, precision=HIGHEST)
                        - jnp.dot(ab_im, cr_ref[d, 1], preferred_element_type=F32, precision=HIGHEST))
                k_scr[d, k] = jnp.where(group_mask, full, 0.0)

        apow_ref[...] = jnp.zeros_like(apow_ref)
        for d in range(2):
            p_re, p_im = _zoh_abar(al_ref[d, 0:1, :], al_ref[d, 1:2, :], al_ref[d, 2:3, :])
            for _ in range(int(math.log2(lc))):
                p_re, p_im = _cmul(p_re, p_im, p_re, p_im)
            for q in range(apow_ref.shape[0]):
                apow_ref[q, 2 * d:2 * d + 1, :] = p_re[:, q * LANES:(q + 1) * LANES]
                apow_ref[q, 2 * d + 1:2 * d + 2, :] = p_im[:, q * LANES:(q + 1) * LANES]

    def widen(x):
        full = _bdot(x.astype(BF16), rep_ref[...])
        return jnp.where(group_mask_wide, full, 0.0)

    part_w = n_g8 * ns
    for sub in range(OP_ROW_BLOCKS):
        s = step * OP_ROW_BLOCKS + sub
        blk_rows = slice(sub * LANES, (sub + 1) * LANES)
        fwd = _cmul(ak_scr[0, lc - 1 - s, 0], ak_scr[0, lc - 1 - s, 1], bb_scr[0, 0], bb_scr[0, 1])
        bwd = _cmul(ak_scr[1, s, 0], ak_scr[1, s, 1], bb_scr[1, 0], bb_scr[1, 1])
        for p, x in enumerate(fwd + bwd):
            wst_ref[blk_rows, p * part_w:(p + 1) * part_w] = widen(x).astype(BF16)

        cf_re, cf_im = _cmul(c2_ref[0, 0], c2_ref[0, 1], ak_scr[0, s + 1, 0], ak_scr[0, s + 1, 1])
        cb_re, cb_im = _cmul(c2_ref[1, 0], c2_ref[1, 1], ak_scr[1, lc - s, 0], ak_scr[1, lc - s, 1])
        for p, x in enumerate((cf_re, -cf_im, cb_re, -cb_im)):
            wout_ref[p * part_w:(p + 1) * part_w, blk_rows] = widen(x).T.astype(BF16)

        for t in range(lc):
            fwd_blk = k_scr[0, jnp.clip(t - s, 0, lc - 1)]
            bwd_blk = k_scr[1, jnp.clip(s - t, 0, lc - 1)]
            blk = jnp.where(t > s, fwd_blk, jnp.where(t < s, bwd_blk, fwd_blk + bwd_blk + skip))
            m_ref[blk_rows, t * LANES:(t + 1) * LANES] = blk.astype(BF16)


def _ssm_operators(a_re, a_im, log_dt, b_re, b_im, c_re, c_im, d_skip):
    depth = a_re.shape[0]
    ng, ns, nj, lc = SSM_GROUPS, SSM_STATE, SSM_GROUP_DIM, SSM_CHUNK
    n_oct, n_g8 = SSM_OCTETS, SSM_GROUPS // SSM_OCTETS
    width = lc * n_g8 * nj
    slabs = SSM_SLABS // n_oct

    def rows_layout(v):
        v = jnp.broadcast_to(v.reshape(depth, 2, n_oct, n_g8, 1, ns), (depth, 2, n_oct, n_g8, nj, ns))
        return v.reshape(depth, 2, n_oct, n_g8 * nj, ns)

    def lanes_layout(v):
        return v.reshape(depth, 2, n_oct, n_g8 * ns)

    dt_gp = jnp.broadcast_to(log_dt[..., None], a_re.shape)
    a_rows = jnp.stack([rows_layout(a_re), rows_layout(a_im), rows_layout(dt_gp)], axis=3)
    a_lanes = jnp.stack([lanes_layout(a_re), lanes_layout(a_im), lanes_layout(dt_gp)], axis=3)
    b_t = jnp.stack([jnp.swapaxes(b_re, -1, -2), jnp.swapaxes(b_im, -1, -2)], axis=3)
    b_t = b_t.reshape(depth, 2, n_oct, n_g8, 2, nj, ns)
    b_t = jnp.moveaxis(b_t, 4, 3).reshape(depth, 2, n_oct, 2, n_g8 * nj, ns)
    c_2 = jnp.stack([c_re, c_im], axis=2).reshape(depth, 2, 2, n_oct, n_g8 * nj, ns)
    c_2 = jnp.moveaxis(c_2, 2, 3)
    c_r = jnp.stack([c_re, c_im], axis=2).reshape(depth, 2, 2, n_oct, n_g8 * nj, ns)
    c_r = jnp.swapaxes(jnp.moveaxis(c_r, 2, 3), -1, -2)
    d_l = d_skip.reshape(depth, n_oct, 1, n_g8 * nj)
    rep = jnp.asarray(np.tile(np.eye(ns), (1, n_g8)), BF16)

    sel = lambda *blk: pl.BlockSpec(blk, lambda l, o, s: (l, 0, o) + (0,) * (len(blk) - 3))
    big = jax.ShapeDtypeStruct((depth, n_oct, width, width), BF16)
    return pl.pallas_call(
        _ssm_op_kernel,
        grid=(depth, n_oct, lc // OP_ROW_BLOCKS),
        in_specs=[
            sel(None, 2, None, 3, n_g8 * nj, ns),
            sel(None, 2, None, 3, n_g8 * ns),
            sel(None, 2, None, 2, n_g8 * nj, ns),
            sel(None, 2, None, 2, n_g8 * nj, ns),
            sel(None, 2, None, 2, ns, n_g8 * nj),
            pl.BlockSpec((None, None, 1, n_g8 * nj), lambda l, o, s: (l, o, 0, 0)),
            pl.BlockSpec(rep.shape, lambda l, o, s: (0, 0)),
        ],
        out_specs=[
            pl.BlockSpec((None, None, OP_ROW_BLOCKS * LANES, width), lambda l, o, s: (l, o, s, 0)),
            pl.BlockSpec((None, None, OP_ROW_BLOCKS * LANES, width), lambda l, o, s: (l, o, s, 0)),
            pl.BlockSpec((None, None, width, OP_ROW_BLOCKS * LANES), lambda l, o, s: (l, o, 0, s)),
            pl.BlockSpec((None, slabs, SUBLANES, LANES), lambda l, o, s: (l, o, 0, 0)),
        ],
        out_shape=[big, big, big,
                   jax.ShapeDtypeStruct((depth, SSM_SLABS, SUBLANES, LANES), F32)],
        scratch_shapes=[
            pltpu.VMEM((2, lc + 1, 2, n_g8 * nj, ns), F32),
            pltpu.VMEM((2, 2, n_g8 * nj, ns), F32),
            pltpu.VMEM((2, lc, LANES, LANES), F32),
        ],
        compiler_params=_cparams("arbitrary", "arbitrary", "arbitrary"),
        name="ssm_operators",
    )(a_rows, a_lanes, b_t, c_2, c_r, d_l, rep)


def _chunk_rows(u_ref):
    return jnp.concatenate([u_ref[t] for t in range(SSM_CHUNK)], axis=1)


def _ssm_state_kernel(u_ref, w_ref, *s_refs):
    s = _bdot(_chunk_rows(u_ref), w_ref[...])
    n_slab = s_refs[0].shape[0]
    for p, ref in enumerate(s_refs):
        for q in range(n_slab):
            col = (p * n_slab + q) * LANES
            ref[q] = s[:, col:col + LANES]


def _ssm_scan_kernel(sfr_ref, sfi_ref, sbr_ref, sbi_ref, a_ref, hfr_ref, hfi_ref, hbr_ref, hbi_ref,
                     *, n_lat_chunks, n_ctx_chunks, batch):
    shape = (batch, LANES)
    coef = [[jnp.broadcast_to(a_ref[k, j:j + 1, :], shape) for j in range(4)]
            for k in range(SCAN_SLABS)]
    zero = jnp.zeros(shape, F32)

    def chain(base, n, carry):
        def step(i, c):
            rf = pl.ds(base + i, batch, stride=n)
            rb = pl.ds(base + (n - 1) - i, batch, stride=n)
            out = []
            for k in range(SCAN_SLABS):
                fr, fi, br, bi = c[4 * k:4 * k + 4]
                afr, afi, abr, abi = coef[k]
                hfr_ref[k, rf, :] = fr
                hfi_ref[k, rf, :] = fi
                hbr_ref[k, rb, :] = br
                hbi_ref[k, rb, :] = bi
                out += [fr * afr - fi * afi + sfr_ref[k, rf, :], fi * afr + fr * afi + sfi_ref[k, rf, :],
                        br * abr - bi * abi + sbr_ref[k, rb, :], bi * abr + br * abi + sbi_ref[k, rb, :]]
            return tuple(out)

        return lax.fori_loop(0, n, step, carry, unroll=2)

    carry = chain(batch * n_lat_chunks, n_ctx_chunks, (zero,) * (4 * SCAN_SLABS))
    chain(0, n_lat_chunks, carry)


def _ssm_out_kernel(u_ref, hfr_ref, hfi_ref, hbr_ref, hbi_ref, m_ref, w_ref, y_ref):
    h = jnp.concatenate([ref[q] for ref in (hfr_ref, hfi_ref, hbr_ref, hbi_ref)
                         for q in range(ref.shape[0])], axis=1)
    y = _bdot(_chunk_rows(u_ref), m_ref[...]) + _bdot(h.astype(BF16), w_ref[...])
    for t in range(SSM_CHUNK):
        y_ref[t] = y[:, t * LANES:(t + 1) * LANES].astype(y_ref.dtype)


def _div_tile(n, limit, mult):
    return max(t for t in range(mult, min(n, limit) + 1, mult) if n % t == 0)


def _ssm_branch(u4, ops, layer, geo):
    m_mat, w_st, w_out, a_pow = ops
    batch, seq, ctx_len = geo["batch"], geo["seq"], geo["ctx"]
    lc = SSM_CHUNK
    n_rows = u4.shape[1]
    width = m_mat.shape[-1]
    n_parts = 4
    oct_slabs = SSM_SLABS // SSM_OCTETS
    bf16_rows = 2 * SUBLANES
    part_shape = jax.ShapeDtypeStruct((SSM_SLABS, n_rows, LANES), F32)

    def u_blk(rt):
        return pl.BlockSpec((lc, rt, LANES), lambda o, r: (0, r, o))

    def part_blk(rt):
        return pl.BlockSpec((oct_slabs, rt, LANES), lambda o, r: (o, r, 0))

    wmat_blk = pl.BlockSpec((None, None, width, width), lambda o, r: (layer, o, 0, 0))

    rt = _div_tile(n_rows, 544, bf16_rows)
    s_parts = pl.pallas_call(
        _ssm_state_kernel,
        grid=(SSM_OCTETS, n_rows // rt),
        in_specs=[u_blk(rt), wmat_blk],
        out_specs=[part_blk(rt)] * n_parts,
        out_shape=[part_shape] * n_parts,
        compiler_params=_cparams("parallel", "parallel"),
        name="ssm_state",
    )(u4, w_st)

    slab_blk = pl.BlockSpec((SCAN_SLABS, n_rows, LANES), lambda q: (q, 0, 0))
    h_parts = pl.pallas_call(
        functools.partial(_ssm_scan_kernel, n_lat_chunks=seq // lc, n_ctx_chunks=ctx_len // lc,
                          batch=batch),
        grid=(SSM_SLABS // SCAN_SLABS,),
        in_specs=[slab_blk] * n_parts + [pl.BlockSpec((None, SCAN_SLABS, SUBLANES, LANES),
                                                      lambda q: (layer, q, 0, 0))],
        out_specs=[slab_blk] * n_parts,
        out_shape=[part_shape] * n_parts,
        compiler_params=_cparams("parallel"),
        name="ssm_scan",
    )(*s_parts, a_pow)

    rt = _div_tile(n_rows, 272, bf16_rows)
    return pl.pallas_call(
        _ssm_out_kernel,
        grid=(SSM_OCTETS, n_rows // rt),
        in_specs=[u_blk(rt)] + [part_blk(rt)] * n_parts + [wmat_blk, wmat_blk],
        out_specs=u_blk(rt),
        out_shape=jax.ShapeDtypeStruct(u4.shape, BF16),
        compiler_params=_cparams("parallel", "parallel"),
        name="ssm_out",
    )(u4, *h_parts, m_mat, w_out)


def _route(scores_t, biased_t):
    sc = [scores_t[e:e + 1, :] for e in range(N_EXPERTS)]
    bc = [biased_t[e:e + 1, :] for e in range(N_EXPERTS)]
    npg = EXPERTS_PER_GROUP

    group_scores = []
    for g in range(N_EXPERT_GROUPS):
        v = bc[g * npg:(g + 1) * npg]
        best = None
        for i in range(npg):
            for j in range(i + 1, npg):
                pair = v[i] + v[j]
                best = pair if best is None else jnp.maximum(best, pair)
        group_scores.append(best)

    bg = jnp.zeros_like(group_scores[0], dtype=jnp.int32)
    best = group_scores[0]
    for g in range(1, N_EXPERT_GROUPS):
        upd = group_scores[g] > best
        bg = jnp.where(upd, g, bg)
        best = jnp.where(upd, group_scores[g], best)

    vb = list(bc[:npg])
    vs = list(sc[:npg])
    for g in range(1, N_EXPERT_GROUPS):
        sel = bg == g
        for j in range(npg):
            vb[j] = jnp.where(sel, bc[g * npg + j], vb[j])
            vs[j] = jnp.where(sel, sc[g * npg + j], vs[j])

    i1 = jnp.zeros_like(bg)
    m1, w1 = vb[0], vs[0]
    for j in range(1, npg):
        upd = vb[j] > m1
        i1 = jnp.where(upd, j, i1)
        m1 = jnp.where(upd, vb[j], m1)
        w1 = jnp.where(upd, vs[j], w1)

    neg = jnp.full_like(m1, -jnp.inf)
    i2 = jnp.zeros_like(bg)
    m2, w2 = neg, jnp.zeros_like(w1)
    for j in range(npg):
        cand = jnp.where(i1 == j, neg, vb[j])
        upd = cand > m2
        i2 = jnp.where(upd, j, i2)
        m2 = jnp.where(upd, cand, m2)
        w2 = jnp.where(upd, vs[j], w2)

    den = w1 + w2
    w1, w2 = w1 / den, w2 / den
    gates = [jnp.where(i1 == j, w1, 0.0) + jnp.where(i2 == j, w2, 0.0) for j in range(npg)]
    return bg, gates


def _merge_kernel(x_ref, y_ref, gp_ref, gs_ref, mod_ref, g2_ref, gluw_ref, glub_ref,
                  wbp_ref, wbs_ref, wo_ref, rw_ref, rb_ref, plat_ref, *rest, lat_tiles):
    if len(rest) == 5:
        pc_ref, x1_ref, h2_ref, grp_ref, ys_ref = rest
        pooled = jnp.where(pl.program_id(0) < lat_tiles, plat_ref[...], pc_ref[...])
    else:
        x1_ref, h2_ref, grp_ref, ys_ref = rest
        pooled = plat_ref[...]
    n_chunks = ys_ref.shape[1] // SSM_CHUNK
    n_col = ys_ref.shape[0]
    for t in range(SSM_CHUNK):
        yt = y_ref[t].astype(F32)
        for j in range(n_col):
            ys_ref[j, pl.ds(t, n_chunks, stride=SSM_CHUNK), :] = yt[:, j * LANES:(j + 1) * LANES]
    s = _gelu_tanh(jnp.concatenate([ys_ref[j] for j in range(n_col)], axis=1))
    s = s * _sigmoid(_bdot(s.astype(BF16), gluw_ref[...]) + glub_ref[...])
    m = (_sigmoid(gp_ref[...].astype(F32)) * _bdot(pooled, wbp_ref[...])
         + _sigmoid(gs_ref[...].astype(F32)) * _bdot(s.astype(BF16), wbs_ref[...]))
    x1 = x_ref[...] + mod_ref[2:3, :] * _bdot(m.astype(BF16), wo_ref[...])
    x1_ref[...] = x1
    h2 = _rms(x1, g2_ref[...]) * (1.0 + mod_ref[4:5, :]) + mod_ref[3:4, :]
    tm, d = h2.shape
    n_chunk = d // LANES + 1
    for c in range(n_chunk - 1):
        h2_ref[pl.ds(c, tm, stride=n_chunk), :] = h2[:, c * LANES:(c + 1) * LANES]

    hi, lo = _split_bf16(h2)
    r1 = _bdot(hi, rw_ref[...])
    logits = r1[:, 0:LANES] + (r1[:, LANES:] + _bdot(lo, rw_ref[:, 0:LANES]))
    scores = _sigmoid(logits)
    bg, gates = _route(scores.T, (scores + rb_ref[...]).T)
    row = lax.broadcasted_iota(jnp.int32, (SUBLANES, tm), 0)
    grp_ref[...] = jnp.broadcast_to(bg, (SUBLANES, tm))
    g8 = jnp.zeros((SUBLANES, tm), F32)
    for j, g in enumerate(gates):
        g8 = jnp.where(row == j, g, g8)
    g_t = jnp.concatenate([g8, jnp.zeros((LANES - SUBLANES, tm), F32)], axis=0)
    h2_ref[pl.ds(n_chunk - 1, tm, stride=n_chunk), :] = g_t.T


def _merge(x_all, y4, pooled, g_pool, g_ssm, mod_l, g2, wts, router, n_rows, geo):
    d = x_all.shape[1]
    tm = geo["tm"]
    n_chunk = d // LANES + 1
    lat_tiles = geo["t_lat"] // tm
    mod_map = lambda i: (_mod_index(i, lat_tiles, geo["seq"] // tm, geo["batch"]), 0, 0)
    row_map = lambda i: (i, 0)
    full = lambda a: pl.BlockSpec(a.shape, lambda i: (0,) * a.ndim)
    glu_w, glu_b, w_bp, w_bs, w_o = wts
    rw, rb = router
    p_lat, p_ctx = pooled
    pooled_specs = [pl.BlockSpec((tm, POOL_WIDTH), lambda i: (jnp.minimum(i, lat_tiles - 1), 0))]
    pooled_args = [p_lat]
    if p_ctx is not None:
        pooled_specs.append(pl.BlockSpec((tm, POOL_WIDTH), lambda i: (jnp.maximum(i - lat_tiles, 0), 0)))
        pooled_args.append(p_ctx)
    return pl.pallas_call(
        functools.partial(_merge_kernel, lat_tiles=lat_tiles),
        grid=(n_rows // tm,),
        in_specs=[
            pl.BlockSpec((tm, d), row_map),
            pl.BlockSpec((SSM_CHUNK, tm // SSM_CHUNK, SSM_WIDTH), lambda i: (0, i, 0)),
            pl.BlockSpec((tm, d), row_map),
            pl.BlockSpec((tm, d), row_map),
            pl.BlockSpec((None, MOD_ROWS, d), mod_map),
            full(g2), full(glu_w), full(glu_b), full(w_bp), full(w_bs), full(w_o), full(rw), full(rb),
        ] + pooled_specs,
        out_specs=[
            pl.BlockSpec((tm, d), row_map),
            pl.BlockSpec((tm * n_chunk, LANES), row_map),
            pl.BlockSpec((SUBLANES, tm), lambda i: (0, i)),
        ],
        out_shape=[
            jax.ShapeDtypeStruct((n_rows, d), F32),
            jax.ShapeDtypeStruct((n_rows * n_chunk, LANES), F32),
            jax.ShapeDtypeStruct((SUBLANES, n_rows), jnp.int32),
        ],
        scratch_shapes=[pltpu.VMEM((SSM_WIDTH // LANES, tm, LANES), F32)],
        compiler_params=_cparams("parallel"),
        name="merge_router",
    )(x_all, y4, g_pool, g_ssm, mod_l, g2, glu_w, glu_b, w_bp, w_bs, w_o, rw, rb, *pooled_args)


def _moe_plan(group, tile):
    n_rows = group.shape[0]
    n_tiles = n_rows // tile + N_EXPERT_GROUPS
    onehot = (group[:, None] == jnp.arange(N_EXPERT_GROUPS)[None, :]).astype(jnp.int32)
    counts = jnp.sum(onehot, axis=0)
    padded = ((counts + tile - 1) // tile) * tile
    ends = jnp.cumsum(padded)
    starts = ends - padded
    rank = jnp.sum((jnp.cumsum(onehot, axis=0) - 1) * onehot, axis=1)
    pos = jnp.sum(onehot * starts[None, :], axis=1) + rank
    token = jnp.arange(n_rows, dtype=jnp.int32)
    slot = jnp.arange(n_tiles * tile, dtype=jnp.int32)
    src = jnp.zeros((n_tiles * tile,), jnp.int32).at[pos].set(
        token, unique_indices=True, mode="promise_in_bounds")
    slot_group = jnp.minimum(jnp.sum(slot[:, None] >= ends[None, :], axis=1), N_EXPERT_GROUPS - 1)
    used = slot < (starts + counts)[slot_group]
    dst = jnp.where(used, src, n_rows + slot % tile)
    tile_start = jnp.arange(n_tiles, dtype=jnp.int32) * tile
    tile_group = jnp.minimum(jnp.sum(tile_start[:, None] >= ends[None, :], axis=1),
                             N_EXPERT_GROUPS - 1).astype(jnp.int32)
    n_used = (ends[-1] // tile).astype(jnp.int32).reshape(1)
    shape3 = (n_tiles, 1, tile)
    return tile_group, n_used, src.reshape(shape3), dst.reshape(shape3)


def _moe_kernel(tg_ref, nu_ref, src_ref, nxt_ref, dst_ref, h_hbm, w1_ref, w3_ref, w2_ref, y_hbm,
                hbuf, ybuf, gsem, ssem, *, tile, n_in, n_out):
    i = pl.program_id(0)
    n_used = nu_ref[0]
    slot = i % 2

    def row_copy_in(idx_ref, r, s):
        return pltpu.make_async_copy(h_hbm.at[pl.ds(idx_ref[0, r], n_in)],
                                     hbuf.at[s, pl.ds(r * n_in, n_in)], gsem.at[s])

    def row_copy_out(r, s):
        first = pl.multiple_of(dst_ref[0, r], n_out)
        return pltpu.make_async_copy(ybuf.at[s, pl.ds(pl.multiple_of(r * n_out, n_out), n_out)],
                                     y_hbm.at[pl.ds(first, n_out)], ssem.at[s])

    def tile_copy_in(s):
        return pltpu.make_async_copy(h_hbm.at[pl.ds(0, tile * n_in)], hbuf.at[s], gsem.at[s])

    def tile_copy_out(s):
        return pltpu.make_async_copy(ybuf.at[s], y_hbm.at[pl.ds(0, tile * n_out)], ssem.at[s])

    def start_rows(make, priority=0):
        def body(r, carry):
            make(r).start(priority=priority)
            return carry
        lax.fori_loop(0, tile, body, 0, unroll=16)

    @pl.when(i == 0)
    def _():
        start_rows(lambda r: row_copy_in(src_ref, r, 0))
        ybuf[1] = jnp.zeros(ybuf.shape[1:], F32)
        spare = pltpu.make_async_copy(
            ybuf.at[1], y_hbm.at[pl.ds(y_hbm.shape[0] - tile * n_out, tile * n_out)], ssem.at[1])
        spare.start()
        spare.wait()

    @pl.when(i + 1 < n_used)
    def _():
        start_rows(lambda r: row_copy_in(nxt_ref, r, 1 - slot))

    @pl.when(i < n_used)
    def _():
        tile_copy_in(slot).wait()
        hrows, yrows = hbuf.at[slot], ybuf.at[slot]
        h = jnp.concatenate([hrows[pl.ds(c, tile, stride=n_in), :] for c in range(n_out)],
                            axis=1).astype(BF16)
        gates = hrows[pl.ds(n_in - 1, tile, stride=n_in), :]
        acc = None
        for e in range(EXPERTS_PER_GROUP):
            hid = (_silu(_bdot(h, w1_ref[e])) * _bdot(h, w3_ref[e])).astype(BF16)
            y = gates[:, e:e + 1] * _bdot(hid, w2_ref[e])
            acc = y if acc is None else acc + y

        @pl.when(i >= 2)
        def _():
            tile_copy_out(slot).wait()

        for c in range(n_out):
            yrows[pl.ds(c, tile, stride=n_out), :] = acc[:, c * LANES:(c + 1) * LANES]
        start_rows(lambda r: row_copy_out(r, slot), priority=1)

    @pl.when(i == pl.num_programs(0) - 1)
    def _():
        tile_copy_out((n_used - 1) % 2).wait()

        @pl.when(n_used >= 2)
        def _():
            tile_copy_out(n_used % 2).wait()


def _moe(h2g, group, w1, w3, w2, geo):
    npg = EXPERTS_PER_GROUP
    d, d_exp = w1.shape[-2:]
    n_out = d // LANES
    n_in = n_out + 1
    n_rows = h2g.shape[0] // n_in
    tile = geo["tm_moe"]
    tile_group, n_used, src, dst = _moe_plan(group, tile)
    src, dst = src * n_in, dst * n_out
    n_tiles = tile_group.shape[0]
    idx_blk = lambda f: pl.BlockSpec((None, 1, tile), f, memory_space=pltpu.SMEM)
    w_map = lambda i, tg, nu: (tg[i], 0, 0, 0)
    grid_spec = pltpu.PrefetchScalarGridSpec(
        num_scalar_prefetch=2,
        grid=(n_tiles,),
        in_specs=[
            idx_blk(lambda i, tg, nu: (i, 0, 0)),
            idx_blk(lambda i, tg, nu: (jnp.minimum(i + 1, n_tiles - 1), 0, 0)),
            idx_blk(lambda i, tg, nu: (i, 0, 0)),
            pl.BlockSpec(memory_space=pl.ANY),
            pl.BlockSpec((None, npg, d, d_exp), w_map),
            pl.BlockSpec((None, npg, d, d_exp), w_map),
            pl.BlockSpec((None, npg, d_exp, d), w_map),
        ],
        out_specs=pl.BlockSpec(memory_space=pl.ANY),
        scratch_shapes=[
            pltpu.VMEM((2, tile * n_in, LANES), F32),
            pltpu.VMEM((2, tile * n_out, LANES), F32),
            pltpu.SemaphoreType.DMA((2,)),
            pltpu.SemaphoreType.DMA((2,)),
        ],
    )
    return pl.pallas_call(
        functools.partial(_moe_kernel, tile=tile, n_in=n_in, n_out=n_out),
        grid_spec=grid_spec,
        out_shape=jax.ShapeDtypeStruct(((n_rows + tile) * n_out, LANES), F32),
        compiler_params=_cparams("arbitrary"),
        name="moe_grouped",
    )(tile_group, n_used, src, src, dst, h2g, w1, w3, w2)


def _residual_kernel(x1_ref, y_ref, mod_ref, fg_ref, o_ref):
    x2 = x1_ref[...] + mod_ref[5:6, :] * _token_rows(y_ref, x1_ref.shape[0])
    o_ref[...] = _rms(x2, fg_ref[...])


def _moe_residual(x1, y_moe, mod_l, final_g, geo):
    n_rows, d = x1.shape
    tm = geo["tm"]
    n_out = d // LANES
    mod_map = lambda i: (_mod_index(i, geo["t_lat"] // tm, geo["seq"] // tm, geo["batch"]), 0, 0)
    row_map = lambda i: (i, 0)
    return pl.pallas_call(
        _residual_kernel,
        grid=(n_rows // tm,),
        in_specs=[
            pl.BlockSpec((tm, d), row_map),
            pl.BlockSpec((tm * n_out, LANES), row_map),
            pl.BlockSpec((None, MOD_ROWS, d), mod_map),
            pl.BlockSpec((1, d), lambda i: (0, 0)),
        ],
        out_specs=pl.BlockSpec((tm, d), row_map),
        out_shape=jax.ShapeDtypeStruct((n_rows, d), F32),
        compiler_params=_cparams("parallel"),
        name="moe_residual",
    )(x1, y_moe, mod_l, final_g)


def kernel(x, c, ctx, c_ctx, w_mod, b_mod, norm1_g, norm2_g, w_in, pool_w, pool_scale, ssm_a_re, ssm_a_im, ssm_log_dt, ssm_b_re, ssm_b_im, ssm_c_re, ssm_c_im, ssm_d, glu_w, glu_b, w_branch_pool, w_branch_ssm, w_out, router_w, router_b, expert_w1, expert_w3, expert_w2, final_g):
    batch, seq, d = x.shape
    ctx_len = ctx.shape[1]
    depth = w_mod.shape[0]
    t_lat, t_ctx = batch * seq, batch * ctx_len
    assert seq % GRID_W == 0 and seq % POOL_TILE == 0 and ctx_len % SSM_CHUNK == 0
    assert t_lat % ctx_len == 0 and batch == SUBLANES and ctx_len <= POOL_TILE
    geo = dict(batch=batch, seq=seq, ctx=ctx_len, t_lat=t_lat,
               tm=_pow2_tile(512, seq, t_ctx), tm_moe=_pow2_tile(512, seq, t_ctx))

    n_cond = 2 * SUBLANES
    c_rows = jnp.concatenate([c, c_ctx[None, :], jnp.zeros((n_cond - batch - 1, d), F32)], axis=0)
    mod = _adaln_tables(c_rows, w_mod, b_mod)

    rw = jnp.pad(router_w, ((0, 0), (0, LANES - N_EXPERTS)))
    rw_hi, rw_lo = _split_bf16(rw)
    rw_cat = jnp.concatenate([rw_hi, rw_lo], axis=1)
    rb = jnp.pad(router_b, (0, LANES - N_EXPERTS)).reshape(1, LANES)
    fg = final_g.reshape(1, d)
    d_exp = expert_w1.shape[-1]
    grouped = lambda w, a, b: w.astype(BF16).reshape(N_EXPERT_GROUPS, EXPERTS_PER_GROUP, a, b)

    ssm_ops = _ssm_operators(ssm_a_re, ssm_a_im, ssm_log_dt, ssm_b_re, ssm_b_im, ssm_c_re, ssm_c_im,
                             ssm_d)
    stream = (x.reshape(t_lat, d), ctx.reshape(t_ctx, d))
    for i in range(depth):
        last = i == depth - 1
        n_rows = t_lat if last else t_lat + t_ctx
        x_all, u_pool, u4, g_pool, g_ssm = _inproj(
            stream, mod[i], norm1_g[i].reshape(1, d), w_in[i].astype(BF16), geo)
        pooled = _pool_branch(u_pool, pool_w[i].astype(BF16), pool_scale[i], not last, geo)
        y4 = _ssm_branch(u4, ssm_ops, i, geo)
        wts = (glu_w[i].astype(BF16), glu_b[i].reshape(1, SSM_WIDTH), w_branch_pool[i].astype(BF16),
               w_branch_ssm[i].astype(BF16), w_out[i].astype(BF16))
        x1, h2g, grp = _merge(x_all, y4, pooled, g_pool, g_ssm, mod[i],
                              norm2_g[i].reshape(1, d), wts, (rw_cat, rb), n_rows, geo)
        y_moe = _moe(h2g, grp[0], grouped(expert_w1[i], d, d_exp), grouped(expert_w3[i], d, d_exp),
                     grouped(expert_w2[i], d_exp, d), geo)
        stream = (x1, y_moe, mod[i])
    return _moe_residual(x1, y_moe, mod[depth - 1], fg, geo).reshape(batch, seq, d)
```
